```python
import jax, jax.numpy as jnp
from jax import lax
import numpy as np

D_MODEL = 2048
BATCH = 4
SEQ = 4096
DEPTH = 2

N_MEM = 256
N_MIXERS = 2
N_A = (DEPTH + 1) // 2
N_B = DEPTH // 2
E_BRANCH = 2 * D_MODEL
E_CA = E_BRANCH // 4
E_MIX = E_BRANCH - E_CA
CA_HEADS = 4
CA_HEAD_DIM = E_CA // CA_HEADS
POOL_WINDOWS = (2, 4, 8, 16)
N_POOL_GROUPS = len(POOL_WINDOWS)
POOL_GROUP = E_MIX // N_POOL_GROUPS
HG_HEAD_DIM = 128
HG_HEADS = E_MIX // HG_HEAD_DIM
HG_CHUNK = 64
EPS = 1e-6

kernel_name = "interleaved_pool_hgrn2_memory_hybrid"


def rmsnorm(x, g):
    xf = x.astype(jnp.float32)
    y = xf * lax.rsqrt(jnp.mean(xf * xf, axis=-1, keepdims=True) + EPS)
    return (y * g.astype(jnp.float32)).astype(x.dtype)


def pool_mixer(u, w_grp, scale):
    B, S, _ = u.shape
    wmax = max(POOL_WINDOWS)
    uf = u.astype(jnp.float32)
    c = jnp.cumsum(uf, axis=1)
    c_pad = jnp.pad(c, ((0, 0), (wmax, 0), (0, 0)))
    pos = jnp.arange(S, dtype=jnp.float32)[:, None]
    pooled = []
    for j, w in enumerate(POOL_WINDOWS):
        lo, hi = j * POOL_GROUP, (j + 1) * POOL_GROUP
        win_sum = c[:, :, lo:hi] - c_pad[:, wmax - w: wmax - w + S, lo:hi]
        count = jnp.minimum(pos + 1.0, float(w))
        pooled.append(win_sum / count - uf[:, :, lo:hi])
    pooled = jnp.stack(pooled, axis=2).astype(u.dtype)
    y = jnp.einsum('bsgc,gcd->bsgd', pooled, w_grp)
    return y.reshape(B, S, E_MIX) * scale


def hgrn2_mixer(q_in, f_in, i_in, lb, norm_g):
    B, S, _ = q_in.shape
    N = S // HG_CHUNK
    f32 = jnp.float32

    def heads(t):
        return t.reshape(B, N, HG_CHUNK, HG_HEADS, HG_HEAD_DIM).transpose(0, 3, 1, 2, 4)

    q = heads(jax.nn.silu(q_in.astype(f32))) * (HG_HEAD_DIM ** -0.5)
    f = lb + (1.0 - lb) * jax.nn.sigmoid(f_in.astype(f32))
    k = heads(1.0 - f)
    b = jnp.cumsum(heads(jnp.log(f)), axis=3)
    v = heads(i_in.astype(f32))
    b_last = b[:, :, :, -1:, :]

    q_dec = q * jnp.exp(b)
    k_inv = k * jnp.exp(-b)
    k_to_end = k * jnp.exp(b_last - b)

    causal = jnp.tril(jnp.ones((HG_CHUNK, HG_CHUNK), dtype=bool))
    attn = jnp.einsum('bhnck,bhnsk->bhncs', q_dec, k_inv)
    attn = jnp.where(causal, attn, 0.0)
    o_intra = jnp.einsum('bhncs,bhnsv->bhncv', attn, v)

    def step(state, xs):
        qd, kte, vv, dec = xs
        o = jnp.einsum('bhck,bhkv->bhcv', qd, state)
        state = dec[..., None] * state + jnp.einsum('bhck,bhcv->bhkv', kte, vv)
        return state, o

    mv = lambda t: jnp.moveaxis(t, 2, 0)
    s0 = jnp.zeros((B, HG_HEADS, HG_HEAD_DIM, HG_HEAD_DIM), f32)
    _, o_inter = lax.scan(step, s0, (mv(q_dec), mv(k_to_end), mv(v), mv(jnp.exp(b_last[:, :, :, 0, :]))))
    o = o_intra + jnp.moveaxis(o_inter, 0, 2)
    o = o * lax.rsqrt(jnp.mean(o * o, axis=-1, keepdims=True) + EPS)
    o = o.transpose(0, 2, 3, 1, 4).reshape(B, S, E_MIX) * norm_g.astype(f32)
    return o.astype(q_in.dtype)


def memory_attention(q_in, mem_n, w_kv):
    B, S, _ = q_in.shape
    M = mem_n.shape[1]
    kv = jnp.einsum('bmd,de->bme', mem_n, w_kv)
    k, v = jnp.split(kv, 2, axis=-1)
    q = q_in.reshape(B, S, CA_HEADS, CA_HEAD_DIM)
    k = k.reshape(B, M, CA_HEADS, CA_HEAD_DIM)
    v = v.reshape(B, M, CA_HEADS, CA_HEAD_DIM)
    s = jnp.einsum('bshd,bmhd->bhsm', q, k).astype(jnp.float32) * (CA_HEAD_DIM ** -0.5)
    p = jax.nn.softmax(s, axis=-1).astype(v.dtype)
    o = jnp.einsum('bhsm,bmhd->bshd', p, v)
    return o.reshape(B, S, E_CA)


def setup_inputs(seed: int = 0) -> dict:
    key = jax.random.key(seed)
    ks = jax.random.split(key, 14)
    f32 = jnp.float32
    nrm = lambda k, shape, s: jax.random.normal(k, shape, f32) * s
    d_in_pool = E_MIX + E_CA + E_BRANCH
    d_in_hg = 3 * E_MIX + E_CA + E_BRANCH
    return {
        "x": nrm(ks[0], (BATCH, SEQ, D_MODEL), 1.0),
        "mem": nrm(ks[1], (BATCH, N_MEM, D_MODEL), 1.0),
        "norm_g": 1.0 + nrm(ks[2], (DEPTH, D_MODEL), 0.02),
        "mem_norm_g": 1.0 + nrm(ks[3], (D_MODEL,), 0.02),
        "w_kv": nrm(ks[4], (DEPTH, D_MODEL, 2 * E_CA), D_MODEL ** -0.5),
        "w_out": nrm(ks[5], (DEPTH, E_BRANCH, D_MODEL), E_BRANCH ** -0.5),
        "pool_w_in": nrm(ks[6], (N_A, D_MODEL, d_in_pool), D_MODEL ** -0.5),
        "pool_w_grp": nrm(ks[7], (N_A, N_POOL_GROUPS, POOL_GROUP, POOL_GROUP), POOL_GROUP ** -0.5),
        "pool_scale": 1.0 + nrm(ks[8], (N_A, E_MIX), 0.1),
        "hgrn_w_in": nrm(ks[9], (N_B, D_MODEL, d_in_hg), D_MODEL ** -0.5),
        "hgrn_lb": nrm(ks[10], (DEPTH, E_MIX), 0.1),
        "hgrn_norm_g": 1.0 + nrm(ks[11], (N_B, E_MIX), 0.02),
        "final_g": 1.0 + nrm(ks[12], (D_MODEL,), 0.02),
    }


def reference(x, mem, norm_g, mem_norm_g, w_kv, w_out, pool_w_in, pool_w_grp, pool_scale,
              hgrn_w_in, hgrn_lb, hgrn_norm_g, final_g):
    mem_n = rmsnorm(mem, mem_norm_g)
    sm = jax.nn.softmax(hgrn_lb.astype(jnp.float32), axis=0)
    lb_all = jnp.cumsum(sm, axis=0) - sm[0:1]
    for i in range(DEPTH):
        h = rmsnorm(x, norm_g[i])
        j = i // N_MIXERS
        if i % N_MIXERS == 0:
            proj = jnp.einsum('bsd,de->bse', h, pool_w_in[j])
            u, q_ca, gate = jnp.split(proj, [E_MIX, E_MIX + E_CA], axis=-1)
            mix = pool_mixer(u, pool_w_grp[j], pool_scale[j])
        else:
            proj = jnp.einsum('bsd,de->bse', h, hgrn_w_in[j])
            q_hg, f_hg, i_hg, q_ca, gate = jnp.split(
                proj, [E_MIX, 2 * E_MIX, 3 * E_MIX, 3 * E_MIX + E_CA], axis=-1)
            mix = hgrn2_mixer(q_hg, f_hg, i_hg, lb_all[i], hgrn_norm_g[j])
        ca = memory_attention(q_ca, mem_n, w_kv[i])
        branch = jnp.concatenate([mix, ca], axis=-1) * jax.nn.silu(gate)
        x = x + jnp.einsum('bse,ed->bsd', branch, w_out[i])
    return rmsnorm(x, final_g)
```

```python
import functools

import jax
import jax.numpy as jnp
from jax import lax
from jax.experimental import pallas as pl
from jax.experimental.pallas import tpu as pltpu

F32 = jnp.float32
BF16 = jnp.bfloat16

DEPTH = 2
N_MIXERS = 2
E_MIX = 3072
E_CA = 1024
E_BRANCH = E_MIX + E_CA
CA_HEADS = 4
CA_HEAD_DIM = E_CA // CA_HEADS
POOL_WINDOWS = (2, 4, 8, 16)
POOL_GROUP = E_MIX // len(POOL_WINDOWS)
POOL_HALO = max(POOL_WINDOWS)
HG_HEAD_DIM = 128
HG_CHUNK = 64
EPS = 1e-6

V7X_VMEM_BYTES = 64 * 1024 * 1024
NT_DIMS = (((1,), (1,)), ((), ()))
TN_DIMS = (((0,), (0,)), ((), ()))


def _nbytes(shape, dtype):
    n = 1
    for s in shape:
        n *= s
    return n * jnp.dtype(dtype).itemsize


def _compiler_params(semantics, blocks, scratch=(), temps=0):
    need = 2 * sum(_nbytes(s, d) for s, d in blocks) + sum(_nbytes(s, d) for s, d in scratch) + temps
    limit = min(need + need // 4 + (2 << 20), V7X_VMEM_BYTES - (4 << 20))
    return pltpu.CompilerParams(dimension_semantics=semantics, vmem_limit_bytes=int(limit))


def _silu(x):
    return x * jax.nn.sigmoid(x)


def _rms_rows(x, g):
    ms = jnp.mean(x * x, axis=-1, keepdims=True)
    return (x * lax.rsqrt(ms + EPS)) * g


def _norm_matmul_kernel(x_ref, g_ref, w_ref, *refs, seg_blocks, row_chunk):
    out_refs, h_ref = refs[:-1], refs[-1]
    j = pl.program_id(1)

    @pl.when(j == 0)
    def _():
        def body(r, carry):
            rows = pl.ds(pl.multiple_of(r * row_chunk, row_chunk), row_chunk)
            h_ref[rows, :] = _rms_rows(x_ref[rows, :], g_ref[...]).astype(BF16)
            return carry
        lax.fori_loop(0, x_ref.shape[0] // row_chunk, body, 0)

    acc = jnp.dot(h_ref[...], w_ref[...], preferred_element_type=F32)
    lo = 0
    for o_ref, nb in zip(out_refs, seg_blocks):
        @pl.when((j >= lo) & (j < lo + nb))
        def _(o_ref=o_ref):
            o_ref[...] = acc.astype(o_ref.dtype)
        lo += nb


def _norm_matmul(x, g, w, segs, *, tm, tn, name):
    M, D = x.shape
    N = w.shape[1]
    assert M % tm == 0 and N % tn == 0 and sum(s[0] for s in segs) == N
    seg_blocks = []
    out_specs, out_shapes = [], []
    lo = 0
    for width, dtype in segs:
        assert width % tn == 0
        nb = width // tn
        out_specs.append(pl.BlockSpec(
            (tm, tn), lambda i, j, lo=lo, nb=nb: (i, jnp.clip(j - lo, 0, nb - 1))))
        out_shapes.append(jax.ShapeDtypeStruct((M, width), dtype))
        seg_blocks.append(nb)
        lo += nb
    blocks = [((tm, D), F32), ((1, D), F32), ((D, tn), BF16)] + [((tm, tn), d) for _, d in segs]
    return pl.pallas_call(
        functools.partial(_norm_matmul_kernel, seg_blocks=tuple(seg_blocks), row_chunk=32),
        grid=(M // tm, N // tn),
        in_specs=[pl.BlockSpec((tm, D), lambda i, j: (i, 0)),
                  pl.BlockSpec((1, D), lambda i, j: (0, 0)),
                  pl.BlockSpec((D, tn), lambda i, j: (0, j))],
        out_specs=out_specs,
        out_shape=out_shapes,
        scratch_shapes=[pltpu.VMEM((tm, D), BF16)],
        compiler_params=_compiler_params(("parallel", "arbitrary"), blocks,
                                         [((tm, D), BF16)], temps=_nbytes((tm, tn), F32)),
        name=name,
    )(x, g.reshape(1, D), w)


def _pool_kernel(u_ref, halo_ref, gate_ref, w_ref, scale_ref, o_ref, *, tiles_per_seq):
    ts = u_ref.shape[0]
    t_in_seq = pl.program_id(0) % tiles_per_seq
    pos = t_in_seq * ts + lax.broadcasted_iota(jnp.int32, (ts, POOL_GROUP), 0)
    for g, w in enumerate(POOL_WINDOWS):
        sl = slice(g * POOL_GROUP, (g + 1) * POOL_GROUP)
        ug = u_ref[:, sl]
        halo = jnp.where(t_in_seq == 0, 0.0, halo_ref[:, sl])
        s = jnp.concatenate([halo, ug], axis=0)
        k = 1
        while k < w:
            s = s + pltpu.roll(s, k, axis=0)
            k *= 2
        cnt = jnp.minimum(pos + 1, w).astype(F32)
        pooled = s[POOL_HALO:, :] / cnt - ug
        y = jnp.dot(pooled.astype(BF16), w_ref[g], preferred_element_type=F32) * scale_ref[:, sl]
        o_ref[:, sl] = (y * _silu(gate_ref[:, sl])).astype(BF16)


def _pool_mixer(u, gate, w_grp, scale, *, seq, ts):
    M = u.shape[0]
    assert seq % ts == 0 and ts % POOL_HALO == 0
    blocks = [((ts, E_MIX), F32), ((POOL_HALO, E_MIX), F32), ((ts, E_BRANCH), F32),
              (w_grp.shape, BF16), ((1, E_MIX), F32), ((ts, E_MIX), BF16)]
    return pl.pallas_call(
        functools.partial(_pool_kernel, tiles_per_seq=seq // ts),
        grid=(M // ts,),
        in_specs=[pl.BlockSpec((ts, E_MIX), lambda i: (i, 0)),
                  pl.BlockSpec((POOL_HALO, E_MIX),
                               lambda i: (jnp.maximum(i * (ts // POOL_HALO) - 1, 0), 0)),
                  pl.BlockSpec((ts, E_BRANCH), lambda i: (i, 0)),
                  pl.BlockSpec(w_grp.shape, lambda i: (0, 0, 0)),
                  pl.BlockSpec((1, E_MIX), lambda i: (0, 0))],
        out_specs=pl.BlockSpec((ts, E_MIX), lambda i: (i, 0)),
        out_shape=jax.ShapeDtypeStruct((M, E_MIX), BF16),
        compiler_params=_compiler_params(("parallel",), blocks,
                                         temps=4 * _nbytes((ts + POOL_HALO, POOL_GROUP), F32)),
        name="pool_mixer",
    )(u, u, gate, w_grp, scale.reshape(1, E_MIX))


def _hgrn_kernel(q_ref, f_ref, v_ref, gate_ref, lb_ref, ng_ref, o_ref, st_ref, *, layer, sub):
    heads = q_ref.shape[1] // HG_HEAD_DIM

    @pl.when(pl.program_id(2) == 0)
    def _():
        st_ref[...] = jnp.zeros_like(st_ref)

    sm = jax.nn.softmax(lb_ref[...], axis=0)
    lb = jnp.sum(sm[:layer + 1], axis=0, keepdims=True) - sm[0:1]

    r_i = lax.broadcasted_iota(jnp.int32, (sub, sub), 0)
    c_i = lax.broadcasted_iota(jnp.int32, (sub, sub), 1)
    causal = (r_i // HG_CHUNK == c_i // HG_CHUNK) & (c_i <= r_i)
    ones_tril = jnp.where(causal, 1.0, 0.0).astype(BF16)

    def body(r, carry):
        rows = pl.ds(pl.multiple_of(r * sub, sub), sub)
        qx = q_ref[rows, :]
        vx = v_ref[rows, :]
        qs = _silu(qx) * (HG_HEAD_DIM ** -0.5)
        fg = lb + (1.0 - lb) * jax.nn.sigmoid(f_ref[rows, :])
        kk = 1.0 - fg
        lf = jnp.log(fg)
        hi = lf.astype(BF16)
        r1 = lf - hi.astype(F32)
        mid = r1.astype(BF16)
        lo = (r1 - mid.astype(F32)).astype(BF16)
        b = (jnp.dot(ones_tril, hi, preferred_element_type=F32)
             + jnp.dot(ones_tril, mid, preferred_element_type=F32)
             + jnp.dot(ones_tril, lo, preferred_element_type=F32))
        qd = (qs * jnp.exp(b)).astype(BF16)
        ki = kk * jnp.exp(-b)
        ki16 = ki.astype(BF16)
        outs = []
        for h in range(heads):
            sl = slice(h * HG_HEAD_DIM, (h + 1) * HG_HEAD_DIM)
            a = lax.dot_general(qd[:, sl], ki16[:, sl], NT_DIMS, preferred_element_type=F32)
            a = jnp.where(causal, a, 0.0).astype(BF16)
            o_h = jnp.dot(a, vx[:, sl], preferred_element_type=F32)
            st = st_ref[h]
            inter = []
            for c in range(sub // HG_CHUNK):
                rs = slice(c * HG_CHUNK, (c + 1) * HG_CHUNK)
                inter.append(lax.dot_general(qd[rs, sl], st.astype(BF16), NT_DIMS,
                                             preferred_element_type=F32))
                dec = jnp.exp(b[(c + 1) * HG_CHUNK - 1:(c + 1) * HG_CHUNK, sl])
                k_to_end = (ki[rs, sl] * dec).astype(BF16)
                st = st * dec + lax.dot_general(vx[rs, sl], k_to_end, TN_DIMS,
                                                preferred_element_type=F32)
            st_ref[h] = st
            o_h = o_h + jnp.concatenate(inter, axis=0)
            ms = jnp.mean(o_h * o_h, axis=-1, keepdims=True)
            outs.append(o_h * lax.rsqrt(ms + EPS))
        mix = jnp.concatenate(outs, axis=1) * ng_ref[...]
        o_ref[rows, :] = (mix * _silu(gate_ref[rows, :])).astype(BF16)
        return carry

    lax.fori_loop(0, q_ref.shape[0] // sub, body, 0)


def _hgrn_mixer(q, f, v, gate, lb_all, norm_g, *, layer, batch, seq, ts, heads_per_step, sub):
    M = q.shape[0]
    W = heads_per_step * HG_HEAD_DIM
    assert seq % ts == 0 and ts % sub == 0 and sub % HG_CHUNK == 0 and E_MIX % W == 0
    spt = seq // ts
    row_map = lambda b, h, s: (b * spt + s, h)
    blocks = [((ts, W), F32), ((ts, W), F32), ((ts, W), BF16), ((ts, W), F32),
              ((DEPTH, W), F32), ((1, W), F32), ((ts, W), BF16)]
    scratch = [((heads_per_step, HG_HEAD_DIM, HG_HEAD_DIM), F32)]
    return pl.pallas_call(
        functools.partial(_hgrn_kernel, layer=layer, sub=sub),
        grid=(batch, E_MIX // W, spt),
        in_specs=[pl.BlockSpec((ts, W), row_map),
                  pl.BlockSpec((ts, W), row_map),
                  pl.BlockSpec((ts, W), row_map),
                  pl.BlockSpec((ts, W), row_map),
                  pl.BlockSpec((DEPTH, W), lambda b, h, s: (0, h)),
                  pl.BlockSpec((1, W), lambda b, h, s: (0, h))],
        out_specs=pl.BlockSpec((ts, W), row_map),
        out_shape=jax.ShapeDtypeStruct((M, E_MIX), BF16),
        scratch_shapes=[pltpu.VMEM(s, d) for s, d in scratch],
        compiler_params=_compiler_params(("parallel", "parallel", "arbitrary"), blocks, scratch,
                                         temps=16 * _nbytes((sub, W), F32)),
        name="hgrn_mixer",
    )(q, f, v, gate, lb_all, norm_g.reshape(1, E_MIX))


def _mem_attn_kernel(q_ref, kv_ref, gate_ref, o_ref):
    for h in range(CA_HEADS):
        sl = slice(h * CA_HEAD_DIM, (h + 1) * CA_HEAD_DIM)
        s = lax.dot_general(q_ref[:, sl], kv_ref[:, sl], NT_DIMS,
                            preferred_element_type=F32) * (CA_HEAD_DIM ** -0.5)
        e = jnp.exp(s - jnp.max(s, axis=-1, keepdims=True))
        p = e / jnp.sum(e, axis=-1, keepdims=True)
        o = jnp.dot(p.astype(BF16), kv_ref[:, E_CA + h * CA_HEAD_DIM:E_CA + (h + 1) * CA_HEAD_DIM],
                    preferred_element_type=F32)
        o_ref[:, sl] = (o * _silu(gate_ref[:, sl])).astype(BF16)


def _mem_attention(q_ca, kv, gate, *, seq, n_mem, ts):
    M = q_ca.shape[0]
    assert seq % ts == 0 and E_MIX % E_CA == 0
    tiles_per_seq = seq // ts
    blocks = [((ts, E_CA), BF16), ((n_mem, 2 * E_CA), BF16), ((ts, E_CA), F32), ((ts, E_CA), BF16)]
    return pl.pallas_call(
        _mem_attn_kernel,
        grid=(M // ts,),
        in_specs=[pl.BlockSpec((ts, E_CA), lambda i: (i, 0)),
                  pl.BlockSpec((n_mem, 2 * E_CA), lambda i: (i // tiles_per_seq, 0)),
                  pl.BlockSpec((ts, E_CA), lambda i: (i, E_MIX // E_CA))],
        out_specs=pl.BlockSpec((ts, E_CA), lambda i: (i, 0)),
        out_shape=jax.ShapeDtypeStruct((M, E_CA), BF16),
        compiler_params=_compiler_params(("parallel",), blocks,
                                         temps=6 * _nbytes((ts, n_mem), F32)),
        name="mem_attention",
    )(q_ca, kv, gate)


def _out_proj_kernel(x_ref, bm_ref, bc_ref, wm_ref, wc_ref, o_ref):
    acc = jnp.dot(bm_ref[...], wm_ref[...], preferred_element_type=F32)
    acc = acc + jnp.dot(bc_ref[...], wc_ref[...], preferred_element_type=F32)
    o_ref[...] = x_ref[...] + acc


def _out_proj(x, bm, bc, w_out, *, tm, tn):
    M, D = x.shape
    assert M % tm == 0 and D % tn == 0
    blocks = [((tm, tn), F32), ((tm, E_MIX), BF16), ((tm, E_CA), BF16),
              ((E_MIX, tn), BF16), ((E_CA, tn), BF16), ((tm, tn), F32)]
    return pl.pallas_call(
        _out_proj_kernel,
        grid=(M // tm, D // tn),
        in_specs=[pl.BlockSpec((tm, tn), lambda i, j: (i, j)),
                  pl.BlockSpec((tm, E_MIX), lambda i, j: (i, 0)),
                  pl.BlockSpec((tm, E_CA), lambda i, j: (i, 0)),
                  pl.BlockSpec((E_MIX, tn), lambda i, j: (0, j)),
                  pl.BlockSpec((E_CA, tn), lambda i, j: (E_MIX // E_CA, j))],
        out_specs=pl.BlockSpec((tm, tn), lambda i, j: (i, j)),
        out_shape=jax.ShapeDtypeStruct((M, D), F32),
        compiler_params=_compiler_params(("parallel", "parallel"), blocks,
                                         temps=_nbytes((tm, tn), F32)),
        name="out_proj",
    )(x, bm, bc, w_out, w_out)


def _final_norm_kernel(x_ref, g_ref, o_ref):
    o_ref[...] = _rms_rows(x_ref[...], g_ref[...])


def _final_norm(x, g, *, tm):
    M, D = x.shape
    blocks = [((tm, D), F32), ((1, D), F32), ((tm, D), F32)]
    return pl.pallas_call(
        _final_norm_kernel,
        grid=(M // tm,),
        in_specs=[pl.BlockSpec((tm, D), lambda i: (i, 0)), pl.BlockSpec((1, D), lambda i: (0, 0))],
        out_specs=pl.BlockSpec((tm, D), lambda i: (i, 0)),
        out_shape=jax.ShapeDtypeStruct((M, D), F32),
        compiler_params=_compiler_params(("parallel",), blocks, temps=_nbytes((tm, D), F32)),
        name="final_norm",
    )(x, g.reshape(1, D))


def kernel(x, mem, norm_g, mem_norm_g, w_kv, w_out, pool_w_in, pool_w_grp, pool_scale,
           hgrn_w_in, hgrn_lb, hgrn_norm_g, final_g):
    batch, seq, d_model = x.shape
    n_mem = mem.shape[1]
    x2 = x.reshape(batch * seq, d_model)
    mem2 = mem.reshape(batch * n_mem, d_model)
    for i in range(DEPTH):
        j = i // N_MIXERS
        kv, = _norm_matmul(mem2, mem_norm_g, w_kv[i].astype(BF16), [(2 * E_CA, BF16)],
                           tm=512, tn=1024, name=f"kv_proj{i}")
        if i % N_MIXERS == 0:
            u, q_ca, gate = _norm_matmul(
                x2, norm_g[i], pool_w_in[j].astype(BF16),
                [(E_MIX, F32), (E_CA, BF16), (E_BRANCH, F32)], tm=512, tn=1024, name=f"in_proj{i}")
            bm = _pool_mixer(u, gate, pool_w_grp[j].astype(BF16), pool_scale[j], seq=seq, ts=256)
        else:
            q_hg, f_hg, i_hg, q_ca, gate = _norm_matmul(
                x2, norm_g[i], hgrn_w_in[j].astype(BF16),
                [(E_MIX, F32), (E_MIX, F32), (E_MIX, BF16), (E_CA, BF16), (E_BRANCH, F32)],
                tm=512, tn=1024, name=f"in_proj{i}")
            bm = _hgrn_mixer(q_hg, f_hg, i_hg, gate, hgrn_lb, hgrn_norm_g[j], layer=i,
                             batch=batch, seq=seq, ts=1024, heads_per_step=2, sub=256)
        bc = _mem_attention(q_ca, kv, gate, seq=seq, n_mem=n_mem, ts=512)
        x2 = _out_proj(x2, bm, bc, w_out[i].astype(BF16), tm=1024, tn=512)
    return _final_norm(x2, final_g, tm=512).reshape(batch, seq, d_model)
```

```python
import functools

import jax
import jax.numpy as jnp
from jax import lax
from jax.experimental import pallas as pl
from jax.experimental.pallas import tpu as pltpu

F32 = jnp.float32
BF16 = jnp.bfloat16

DEPTH = 2
N_MIXERS = 2
E_MIX = 3072
E_CA = 1024
E_BRANCH = E_MIX + E_CA
CA_HEADS = 4
CA_HEAD_DIM = E_CA // CA_HEADS
POOL_WINDOWS = (2, 4, 8, 16)
POOL_GROUP = E_MIX // len(POOL_WINDOWS)
POOL_HALO = max(POOL_WINDOWS)
HG_HEAD_DIM = 128
HG_CHUNK = 64
EPS = 1e-6

V7X_VMEM_BYTES = 64 * 1024 * 1024
NT_DIMS = (((1,), (1,)), ((), ()))
TN_DIMS = (((0,), (0,)), ((), ()))


def _nbytes(shape, dtype):
    n = 1
    for s in shape:
        n *= s
    return n * jnp.dtype(dtype).itemsize


def _compiler_params(semantics, blocks, scratch=(), temps=0):
    need = 2 * sum(_nbytes(s, d) for s, d in blocks) + sum(_nbytes(s, d) for s, d in scratch) + temps
    limit = min(need + need // 4 + (2 << 20), V7X_VMEM_BYTES - (4 << 20))
    return pltpu.CompilerParams(dimension_semantics=semantics, vmem_limit_bytes=int(limit))


def _silu(x):
    return x * jax.nn.sigmoid(x)


def _rms_rows(x, g):
    ms = jnp.mean(x * x, axis=-1, keepdims=True)
    return (x * lax.rsqrt(ms + EPS)) * g


def _norm_rows_to(h_ref, x_ref, g_ref, row_chunk):
    def body(r, carry):
        rows = pl.ds(pl.multiple_of(r * row_chunk, row_chunk), row_chunk)
        h_ref[rows, :] = _rms_rows(x_ref[rows, :], g_ref[...]).astype(BF16)
        return carry
    lax.fori_loop(0, x_ref.shape[0] // row_chunk, body, 0)


def _attend(q, k, v):
    s = lax.dot_general(q, k, NT_DIMS, preferred_element_type=F32) * (CA_HEAD_DIM ** -0.5)
    e = jnp.exp(s - jnp.max(s, axis=-1, keepdims=True))
    p = e / jnp.sum(e, axis=-1, keepdims=True)
    return jnp.dot(p.astype(BF16), v, preferred_element_type=F32)


def _norm_matmul_kernel(x_ref, g_ref, w_ref, *refs, seg_blocks, row_chunk):
    out_refs, h_ref = refs[:-1], refs[-1]
    j = pl.program_id(1)

    @pl.when(j == 0)
    def _():
        _norm_rows_to(h_ref, x_ref, g_ref, row_chunk)

    acc = jnp.dot(h_ref[...], w_ref[...], preferred_element_type=F32)
    lo = 0
    for o_ref, nb in zip(out_refs, seg_blocks):
        @pl.when((j >= lo) & (j < lo + nb))
        def _(o_ref=o_ref):
            o_ref[...] = acc.astype(o_ref.dtype)
        lo += nb


def _norm_matmul(x, g, w, segs, *, tm, tn, name):
    M, D = x.shape
    N = w.shape[1]
    assert M % tm == 0 and N % tn == 0 and sum(s[0] for s in segs) == N
    seg_blocks = []
    out_specs, out_shapes = [], []
    lo = 0
    for width, dtype in segs:
        assert width % tn == 0
        nb = width // tn
        out_specs.append(pl.BlockSpec(
            (tm, tn), lambda i, j, lo=lo, nb=nb: (i, jnp.clip(j - lo, 0, nb - 1))))
        out_shapes.append(jax.ShapeDtypeStruct((M, width), dtype))
        seg_blocks.append(nb)
        lo += nb
    blocks = [((tm, D), F32), ((1, D), F32), ((D, tn), BF16)] + [((tm, tn), d) for _, d in segs]
    return pl.pallas_call(
        functools.partial(_norm_matmul_kernel, seg_blocks=tuple(seg_blocks), row_chunk=32),
        grid=(M // tm, N // tn),
        in_specs=[pl.BlockSpec((tm, D), lambda i, j: (i, 0)),
                  pl.BlockSpec((1, D), lambda i, j: (0, 0)),
                  pl.BlockSpec((D, tn), lambda i, j: (0, j))],
        out_specs=out_specs,
        out_shape=out_shapes,
        scratch_shapes=[pltpu.VMEM((tm, D), BF16)],
        compiler_params=_compiler_params(("parallel", "arbitrary"), blocks,
                                         [((tm, D), BF16)], temps=_nbytes((tm, tn), F32)),
        name=name,
    )(x, g.reshape(1, D), w)


def _pool_kernel(u_ref, halo_ref, gate_ref, w_ref, scale_ref, o_ref, *, tiles_per_seq):
    ts = u_ref.shape[0]
    t_in_seq = pl.program_id(0) % tiles_per_seq
    pos = t_in_seq * ts + lax.broadcasted_iota(jnp.int32, (ts, POOL_GROUP), 0)
    for g, w in enumerate(POOL_WINDOWS):
        sl = slice(g * POOL_GROUP, (g + 1) * POOL_GROUP)
        ug = u_ref[:, sl]
        halo = jnp.where(t_in_seq == 0, 0.0, halo_ref[:, sl])
        s = jnp.concatenate([halo, ug], axis=0)
        k = 1
        while k < w:
            s = s + pltpu.roll(s, k, axis=0)
            k *= 2
        cnt = jnp.minimum(pos + 1, w).astype(F32)
        pooled = s[POOL_HALO:, :] / cnt - ug
        y = jnp.dot(pooled.astype(BF16), w_ref[g], preferred_element_type=F32) * scale_ref[:, sl]
        o_ref[:, sl] = (y * _silu(gate_ref[:, sl])).astype(BF16)


def _pool_mixer(u, gate, w_grp, scale, *, seq, ts):
    M = u.shape[0]
    assert seq % ts == 0 and ts % POOL_HALO == 0
    blocks = [((ts, E_MIX), F32), ((POOL_HALO, E_MIX), F32), ((ts, E_BRANCH), F32),
              (w_grp.shape, BF16), ((1, E_MIX), F32), ((ts, E_MIX), BF16)]
    return pl.pallas_call(
        functools.partial(_pool_kernel, tiles_per_seq=seq // ts),
        grid=(M // ts,),
        in_specs=[pl.BlockSpec((ts, E_MIX), lambda i: (i, 0)),
                  pl.BlockSpec((POOL_HALO, E_MIX),
                               lambda i: (jnp.maximum(i * (ts // POOL_HALO) - 1, 0), 0)),
                  pl.BlockSpec((ts, E_BRANCH), lambda i: (i, 0)),
                  pl.BlockSpec(w_grp.shape, lambda i: (0, 0, 0)),
                  pl.BlockSpec((1, E_MIX), lambda i: (0, 0))],
        out_specs=pl.BlockSpec((ts, E_MIX), lambda i: (i, 0)),
        out_shape=jax.ShapeDtypeStruct((M, E_MIX), BF16),
        compiler_params=_compiler_params(("parallel",), blocks,
                                         temps=4 * _nbytes((ts + POOL_HALO, POOL_GROUP), F32)),
        name="pool_mixer",
    )(u, u, gate, w_grp, scale.reshape(1, E_MIX))


def _hgrn_rows(qx, fx, vx, lb, states, causal, ones_tril):
    rows = qx.shape[0]
    qs = _silu(qx) * (HG_HEAD_DIM ** -0.5)
    fg = lb + (1.0 - lb) * jax.nn.sigmoid(fx)
    kk = 1.0 - fg
    lf = jnp.log(fg)
    hi = lf.astype(BF16)
    r1 = lf - hi.astype(F32)
    mid = r1.astype(BF16)
    lo = (r1 - mid.astype(F32)).astype(BF16)
    b = (jnp.dot(ones_tril, hi, preferred_element_type=F32)
         + jnp.dot(ones_tril, mid, preferred_element_type=F32)
         + jnp.dot(ones_tril, lo, preferred_element_type=F32))
    qd = (qs * jnp.exp(b)).astype(BF16)
    ki = kk * jnp.exp(-b)
    ki16 = ki.astype(BF16)
    outs, new_states = [], []
    for h, st in enumerate(states):
        sl = slice(h * HG_HEAD_DIM, (h + 1) * HG_HEAD_DIM)
        a = lax.dot_general(qd[:, sl], ki16[:, sl], NT_DIMS, preferred_element_type=F32)
        a = jnp.where(causal, a, 0.0).astype(BF16)
        o_h = jnp.dot(a, vx[:, sl], preferred_element_type=F32)
        inter = []
        for c in range(rows // HG_CHUNK):
            rs = slice(c * HG_CHUNK, (c + 1) * HG_CHUNK)
            inter.append(lax.dot_general(qd[rs, sl], st.astype(BF16), NT_DIMS,
                                         preferred_element_type=F32))
            dec = jnp.exp(b[(c + 1) * HG_CHUNK - 1:(c + 1) * HG_CHUNK, sl])
            k_to_end = (ki[rs, sl] * dec).astype(BF16)
            st = st * dec + lax.dot_general(vx[rs, sl], k_to_end, TN_DIMS,
                                            preferred_element_type=F32)
        new_states.append(st)
        o_h = o_h + jnp.concatenate(inter, axis=0)
        ms = jnp.mean(o_h * o_h, axis=-1, keepdims=True)
        outs.append(o_h * lax.rsqrt(ms + EPS))
    return jnp.concatenate(outs, axis=1), new_states


def _hgrn_layer_kernel(x_ref, g_ref, w_ref, lb_ref, ng_ref, k_ref, v_ref, bm_ref, bc_ref,
                       h_ref, st_ref, *, layer, sub, n_mix_steps, row_chunk):
    s = pl.program_id(1)
    j = pl.program_id(2)
    tm = x_ref.shape[0]
    W = bm_ref.shape[1]
    heads = W // HG_HEAD_DIM

    @pl.when(j == 0)
    def _():
        _norm_rows_to(h_ref, x_ref, g_ref, row_chunk)

    @pl.when((j == 0) & (s == 0))
    def _():
        st_ref[...] = jnp.zeros_like(st_ref)

    @pl.when(j < n_mix_steps)
    def _():
        sm = jax.nn.softmax(lb_ref[...], axis=0)
        lb = jnp.sum(sm[:layer + 1], axis=0, keepdims=True) - sm[0:1]
        r_i = lax.broadcasted_iota(jnp.int32, (sub, sub), 0)
        c_i = lax.broadcasted_iota(jnp.int32, (sub, sub), 1)
        causal = (r_i // HG_CHUNK == c_i // HG_CHUNK) & (c_i <= r_i)
        ones_tril = jnp.where(causal, 1.0, 0.0).astype(BF16)
        states = [st_ref[j * heads + h] for h in range(heads)]
        for r in range(tm // sub):
            rows = slice(r * sub, (r + 1) * sub)
            acc = jnp.dot(h_ref[rows, :], w_ref[...], preferred_element_type=F32)
            o, states = _hgrn_rows(acc[:, :W], acc[:, W:2 * W], acc[:, 2 * W:3 * W].astype(BF16),
                                   lb, states, causal, ones_tril)
            bm_ref[rows, :] = ((o * ng_ref[...]) * _silu(acc[:, 3 * W:])).astype(BF16)
        for h in range(heads):
            st_ref[j * heads + h] = states[h]

    @pl.when(j >= n_mix_steps)
    def _():
        wc = bc_ref.shape[1]
        for r in range(tm // sub):
            rows = slice(r * sub, (r + 1) * sub)
            acc = jnp.dot(h_ref[rows, :], w_ref[...], preferred_element_type=F32)
            for hh in range(wc // CA_HEAD_DIM):
                sl = slice(hh * CA_HEAD_DIM, (hh + 1) * CA_HEAD_DIM)
                o = _attend(acc[:, sl].astype(BF16), k_ref[:, sl], v_ref[:, sl])
                gate = acc[:, wc + hh * CA_HEAD_DIM:wc + (hh + 1) * CA_HEAD_DIM]
                bc_ref[rows, sl] = (o * _silu(gate)).astype(BF16)


def _hgrn_weight_blocks(w_in, heads_per_step, ca_heads_per_step):
    D = w_in.shape[0]
    W = heads_per_step * HG_HEAD_DIM
    wc = ca_heads_per_step * CA_HEAD_DIM
    gate0 = 3 * E_MIX + E_CA
    mix = [w_in[:, k * E_MIX:(k + 1) * E_MIX].reshape(D, E_MIX // W, W) for k in range(3)]
    mix.append(w_in[:, gate0:gate0 + E_MIX].reshape(D, E_MIX // W, W))
    mix = jnp.stack(mix, axis=2).reshape(D, 4 * E_MIX)
    ca = [w_in[:, 3 * E_MIX:gate0].reshape(D, E_CA // wc, wc),
          w_in[:, gate0 + E_MIX:].reshape(D, E_CA // wc, wc)]
    ca = jnp.stack(ca, axis=2).reshape(D, 2 * E_CA)
    return jnp.concatenate([mix, ca], axis=1).astype(BF16)


def _hgrn_layer(x, g, w_in, lb_all, norm_g, kv, *, layer, batch, seq, n_mem, tm, sub):
    M, D = x.shape
    heads_per_step, ca_heads_per_step = 2, 2
    W = heads_per_step * HG_HEAD_DIM
    wc = ca_heads_per_step * CA_HEAD_DIM
    tn = 4 * W
    assert tn == 2 * wc and seq % tm == 0 and tm % sub == 0 and sub % HG_CHUNK == 0
    n_mix, n_ca = E_MIX // W, E_CA // wc
    spt = seq // tm
    wp = _hgrn_weight_blocks(w_in, heads_per_step, ca_heads_per_step)
    mix_col = lambda b, s, j: (0, jnp.minimum(j, n_mix - 1))
    ca_blk = lambda j: jnp.clip(j - n_mix, 0, n_ca - 1)
    blocks = [((tm, D), F32), ((1, D), F32), ((D, tn), BF16), ((DEPTH, W), F32), ((1, W), F32),
              ((n_mem, wc), BF16), ((n_mem, wc), BF16), ((tm, W), BF16), ((tm, wc), BF16)]
    scratch = [((tm, D), BF16), ((E_MIX // HG_HEAD_DIM, HG_HEAD_DIM, HG_HEAD_DIM), F32)]
    return pl.pallas_call(
        functools.partial(_hgrn_layer_kernel, layer=layer, sub=sub, n_mix_steps=n_mix,
                          row_chunk=32),
        grid=(batch, spt, n_mix + n_ca),
        in_specs=[pl.BlockSpec((tm, D), lambda b, s, j: (b * spt + s, 0)),
                  pl.BlockSpec((1, D), lambda b, s, j: (0, 0)),
                  pl.BlockSpec((D, tn), lambda b, s, j: (0, j)),
                  pl.BlockSpec((DEPTH, W), mix_col),
                  pl.BlockSpec((1, W), mix_col),
                  pl.BlockSpec((n_mem, wc), lambda b, s, j: (b, ca_blk(j))),
                  pl.BlockSpec((n_mem, wc), lambda b, s, j: (b, E_CA // wc + ca_blk(j)))],
        out_specs=[pl.BlockSpec((tm, W), lambda b, s, j: (b * spt + s, jnp.minimum(j, n_mix - 1))),
                   pl.BlockSpec((tm, wc), lambda b, s, j: (b * spt + s, ca_blk(j)))],
        out_shape=[jax.ShapeDtypeStruct((M, E_MIX), BF16), jax.ShapeDtypeStruct((M, E_CA), BF16)],
        scratch_shapes=[pltpu.VMEM(s_, d_) for s_, d_ in scratch],
        compiler_params=_compiler_params(("parallel", "arbitrary", "arbitrary"), blocks, scratch,
                                         temps=8 * _nbytes((sub, tn), F32)),
        name="hgrn_layer",
    )(x, g.reshape(1, D), wp, lb_all, norm_g.reshape(1, E_MIX), kv, kv)


def _mem_attn_kernel(q_ref, kv_ref, gate_ref, o_ref):
    for h in range(CA_HEADS):
        sl = slice(h * CA_HEAD_DIM, (h + 1) * CA_HEAD_DIM)
        o = _attend(q_ref[:, sl], kv_ref[:, sl],
                    kv_ref[:, E_CA + h * CA_HEAD_DIM:E_CA + (h + 1) * CA_HEAD_DIM])
        o_ref[:, sl] = (o * _silu(gate_ref[:, sl])).astype(BF16)


def _mem_attention(q_ca, kv, gate, *, seq, n_mem, ts):
    M = q_ca.shape[0]
    assert seq % ts == 0 and E_MIX % E_CA == 0
    tiles_per_seq = seq // ts
    blocks = [((ts, E_CA), BF16), ((n_mem, 2 * E_CA), BF16), ((ts, E_CA), F32), ((ts, E_CA), BF16)]
    return pl.pallas_call(
        _mem_attn_kernel,
        grid=(M // ts,),
        in_specs=[pl.BlockSpec((ts, E_CA), lambda i: (i, 0)),
                  pl.BlockSpec((n_mem, 2 * E_CA), lambda i: (i // tiles_per_seq, 0)),
                  pl.BlockSpec((ts, E_CA), lambda i: (i, E_MIX // E_CA))],
        out_specs=pl.BlockSpec((ts, E_CA), lambda i: (i, 0)),
        out_shape=jax.ShapeDtypeStruct((M, E_CA), BF16),
        compiler_params=_compiler_params(("parallel",), blocks,
                                         temps=6 * _nbytes((ts, n_mem), F32)),
        name="mem_attention",
    )(q_ca, kv, gate)


def _out_proj_kernel(x_ref, bm_ref, bc_ref, wm_ref, wc_ref, o_ref):
    acc = jnp.dot(bm_ref[...], wm_ref[...], preferred_element_type=F32)
    acc = acc + jnp.dot(bc_ref[...], wc_ref[...], preferred_element_type=F32)
    o_ref[...] = x_ref[...] + acc


def _out_proj(x, bm, bc, w_out, *, tm, tn):
    M, D = x.shape
    assert M % tm == 0 and D % tn == 0
    blocks = [((tm, tn), F32), ((tm, E_MIX), BF16), ((tm, E_CA), BF16),
              ((E_MIX, tn), BF16), ((E_CA, tn), BF16), ((tm, tn), F32)]
    return pl.pallas_call(
        _out_proj_kernel,
        grid=(M // tm, D // tn),
        in_specs=[pl.BlockSpec((tm, tn), lambda i, j: (i, j)),
                  pl.BlockSpec((tm, E_MIX), lambda i, j: (i, 0)),
                  pl.BlockSpec((tm, E_CA), lambda i, j: (i, 0)),
                  pl.BlockSpec((E_MIX, tn), lambda i, j: (0, j)),
                  pl.BlockSpec((E_CA, tn), lambda i, j: (E_MIX // E_CA, j))],
        out_specs=pl.BlockSpec((tm, tn), lambda i, j: (i, j)),
        out_shape=jax.ShapeDtypeStruct((M, D), F32),
        compiler_params=_compiler_params(("parallel", "parallel"), blocks,
                                         temps=_nbytes((tm, tn), F32)),
        name="out_proj",
    )(x, bm, bc, w_out, w_out)


def _final_norm_kernel(x_ref, g_ref, o_ref):
    o_ref[...] = _rms_rows(x_ref[...], g_ref[...])


def _final_norm(x, g, *, tm):
    M, D = x.shape
    blocks = [((tm, D), F32), ((1, D), F32), ((tm, D), F32)]
    return pl.pallas_call(
        _final_norm_kernel,
        grid=(M // tm,),
        in_specs=[pl.BlockSpec((tm, D), lambda i: (i, 0)), pl.BlockSpec((1, D), lambda i: (0, 0))],
        out_specs=pl.BlockSpec((tm, D), lambda i: (i, 0)),
        out_shape=jax.ShapeDtypeStruct((M, D), F32),
        compiler_params=_compiler_params(("parallel",), blocks, temps=_nbytes((tm, D), F32)),
        name="final_norm",
    )(x, g.reshape(1, D))


def kernel(x, mem, norm_g, mem_norm_g, w_kv, w_out, pool_w_in, pool_w_grp, pool_scale,
           hgrn_w_in, hgrn_lb, hgrn_norm_g, final_g):
    batch, seq, d_model = x.shape
    n_mem = mem.shape[1]
    x2 = x.reshape(batch * seq, d_model)
    mem2 = mem.reshape(batch * n_mem, d_model)
    for i in range(DEPTH):
        j = i // N_MIXERS
        kv, = _norm_matmul(mem2, mem_norm_g, w_kv[i].astype(BF16), [(2 * E_CA, BF16)],
                           tm=512, tn=1024, name=f"kv_proj{i}")
        if i % N_MIXERS == 0:
            u, q_ca, gate = _norm_matmul(
                x2, norm_g[i], pool_w_in[j].astype(BF16),
                [(E_MIX, F32), (E_CA, BF16), (E_BRANCH, F32)], tm=512, tn=1024, name=f"in_proj{i}")
            bm = _pool_mixer(u, gate, pool_w_grp[j].astype(BF16), pool_scale[j], seq=seq, ts=256)
            bc = _mem_attention(q_ca, kv, gate, seq=seq, n_mem=n_mem, ts=512)
        else:
            bm, bc = _hgrn_layer(x2, norm_g[i], hgrn_w_in[j], hgrn_lb, hgrn_norm_g[j], kv,
                                 layer=i, batch=batch, seq=seq, n_mem=n_mem, tm=1024, sub=256)
        x2 = _out_proj(x2, bm, bc, w_out[i].astype(BF16), tm=1024, tn=512)
    return _final_norm(x2, final_g, tm=512).reshape(batch, seq, d_model)
```

```python
import functools

import jax
import jax.numpy as jnp
from jax import lax
from jax.experimental import pallas as pl
from jax.experimental.pallas import tpu as pltpu

F32 = jnp.float32
BF16 = jnp.bfloat16

DEPTH = 2
N_MIXERS = 2
E_MIX = 3072
E_CA = 1024
E_BRANCH = E_MIX + E_CA
CA_HEADS = 4
CA_HEAD_DIM = E_CA // CA_HEADS
POOL_WINDOWS = (2, 4, 8, 16)
POOL_GROUP = E_MIX // len(POOL_WINDOWS)
POOL_HALO = max(POOL_WINDOWS)
HG_HEAD_DIM = 128
HG_CHUNK = 64
EPS = 1e-6

V7X_VMEM_BYTES = 64 * 1024 * 1024
NT_DIMS = (((1,), (1,)), ((), ()))
TN_DIMS = (((0,), (0,)), ((), ()))


def _nbytes(shape, dtype):
    n = 1
    for s in shape:
        n *= s
    return n * jnp.dtype(dtype).itemsize


def _compiler_params(semantics, blocks, scratch=(), temps=0):
    need = 2 * sum(_nbytes(s, d) for s, d in blocks) + sum(_nbytes(s, d) for s, d in scratch) + temps
    limit = min(need + need // 4 + (2 << 20), V7X_VMEM_BYTES - (4 << 20))
    return pltpu.CompilerParams(dimension_semantics=semantics, vmem_limit_bytes=int(limit))


def _silu(x):
    return x * jax.nn.sigmoid(x)


def _rms_rows(x, g):
    ms = jnp.mean(x * x, axis=-1, keepdims=True)
    return (x * lax.rsqrt(ms + EPS)) * g


def _norm_rows_to(h_ref, x_ref, g_ref, row_chunk):
    def body(r, carry):
        rows = pl.ds(pl.multiple_of(r * row_chunk, row_chunk), row_chunk)
        h_ref[rows, :] = _rms_rows(x_ref[rows, :], g_ref[...]).astype(BF16)
        return carry
    lax.fori_loop(0, x_ref.shape[0] // row_chunk, body, 0)


def _attend(q, k, v):
    s = lax.dot_general(q, k, NT_DIMS, preferred_element_type=F32) * (CA_HEAD_DIM ** -0.5)
    e = jnp.exp(s - jnp.max(s, axis=-1, keepdims=True))
    p = e / jnp.sum(e, axis=-1, keepdims=True)
    return jnp.dot(p.astype(BF16), v, preferred_element_type=F32)


def _norm_matmul_kernel(x_ref, g_ref, w_ref, *refs, seg_blocks, row_chunk):
    out_refs, h_ref = refs[:-1], refs[-1]
    j = pl.program_id(1)

    @pl.when(j == 0)
    def _():
        _norm_rows_to(h_ref, x_ref, g_ref, row_chunk)

    acc = jnp.dot(h_ref[...], w_ref[...], preferred_element_type=F32)
    lo = 0
    for o_ref, nb in zip(out_refs, seg_blocks):
        @pl.when((j >= lo) & (j < lo + nb))
        def _(o_ref=o_ref):
            o_ref[...] = acc.astype(o_ref.dtype)
        lo += nb


def _norm_matmul(x, g, w, segs, *, tm, tn, name):
    M, D = x.shape
    N = w.shape[1]
    assert M % tm == 0 and N % tn == 0 and sum(s[0] for s in segs) == N
    seg_blocks = []
    out_specs, out_shapes = [], []
    lo = 0
    for width, dtype in segs:
        assert width % tn == 0
        nb = width // tn
        out_specs.append(pl.BlockSpec(
            (tm, tn), lambda i, j, lo=lo, nb=nb: (i, jnp.clip(j - lo, 0, nb - 1))))
        out_shapes.append(jax.ShapeDtypeStruct((M, width), dtype))
        seg_blocks.append(nb)
        lo += nb
    blocks = [((tm, D), F32), ((1, D), F32), ((D, tn), BF16)] + [((tm, tn), d) for _, d in segs]
    return pl.pallas_call(
        functools.partial(_norm_matmul_kernel, seg_blocks=tuple(seg_blocks), row_chunk=32),
        grid=(M // tm, N // tn),
        in_specs=[pl.BlockSpec((tm, D), lambda i, j: (i, 0)),
                  pl.BlockSpec((1, D), lambda i, j: (0, 0)),
                  pl.BlockSpec((D, tn), lambda i, j: (0, j))],
        out_specs=out_specs,
        out_shape=out_shapes,
        scratch_shapes=[pltpu.VMEM((tm, D), BF16)],
        compiler_params=_compiler_params(("parallel", "arbitrary"), blocks,
                                         [((tm, D), BF16)], temps=_nbytes((tm, tn), F32)),
        name=name,
    )(x, g.reshape(1, D), w)


def _pool_kernel(u_ref, halo_ref, gate_ref, w_ref, scale_ref, o_ref, *, tiles_per_seq):
    ts = u_ref.shape[0]
    t_in_seq = pl.program_id(0) % tiles_per_seq
    pos = t_in_seq * ts + lax.broadcasted_iota(jnp.int32, (ts, POOL_GROUP), 0)
    for g, w in enumerate(POOL_WINDOWS):
        sl = slice(g * POOL_GROUP, (g + 1) * POOL_GROUP)
        ug = u_ref[:, sl]
        halo = jnp.where(t_in_seq == 0, 0.0, halo_ref[:, sl])
        s = jnp.concatenate([halo, ug], axis=0)
        k = 1
        while k < w:
            s = s + pltpu.roll(s, k, axis=0)
            k *= 2
        cnt = jnp.minimum(pos + 1, w).astype(F32)
        pooled = s[POOL_HALO:, :] / cnt - ug
        y = jnp.dot(pooled.astype(BF16), w_ref[g], preferred_element_type=F32) * scale_ref[:, sl]
        o_ref[:, sl] = (y * _silu(gate_ref[:, sl])).astype(BF16)


def _pool_mixer(u, gate, w_grp, scale, *, seq, ts):
    M = u.shape[0]
    assert seq % ts == 0 and ts % POOL_HALO == 0
    blocks = [((ts, E_MIX), F32), ((POOL_HALO, E_MIX), F32), ((ts, E_BRANCH), F32),
              (w_grp.shape, BF16), ((1, E_MIX), F32), ((ts, E_MIX), BF16)]
    return pl.pallas_call(
        functools.partial(_pool_kernel, tiles_per_seq=seq // ts),
        grid=(M // ts,),
        in_specs=[pl.BlockSpec((ts, E_MIX), lambda i: (i, 0)),
                  pl.BlockSpec((POOL_HALO, E_MIX),
                               lambda i: (jnp.maximum(i * (ts // POOL_HALO) - 1, 0), 0)),
                  pl.BlockSpec((ts, E_BRANCH), lambda i: (i, 0)),
                  pl.BlockSpec(w_grp.shape, lambda i: (0, 0, 0)),
                  pl.BlockSpec((1, E_MIX), lambda i: (0, 0))],
        out_specs=pl.BlockSpec((ts, E_MIX), lambda i: (i, 0)),
        out_shape=jax.ShapeDtypeStruct((M, E_MIX), BF16),
        compiler_params=_compiler_params(("parallel",), blocks,
                                         temps=4 * _nbytes((ts + POOL_HALO, POOL_GROUP), F32)),
        name="pool_mixer",
    )(u, u, gate, w_grp, scale.reshape(1, E_MIX))


def _hgrn_rows(qx, fx, vx, lb, states, causal, ones_tril, between):
    rows = qx.shape[0]
    qs = _silu(qx) * (HG_HEAD_DIM ** -0.5)
    fg = lb + (1.0 - lb) * jax.nn.sigmoid(fx)
    kk = 1.0 - fg
    lf = jnp.log(fg)
    hi = lf.astype(BF16)
    r1 = lf - hi.astype(F32)
    mid = r1.astype(BF16)
    lo = (r1 - mid.astype(F32)).astype(BF16)
    between(0)
    b = (jnp.dot(ones_tril, hi, preferred_element_type=F32)
         + jnp.dot(ones_tril, mid, preferred_element_type=F32)
         + jnp.dot(ones_tril, lo, preferred_element_type=F32))
    qd = (qs * jnp.exp(b)).astype(BF16)
    ki = kk * jnp.exp(-b)
    ki16 = ki.astype(BF16)
    between(1)
    heads = range(len(states))
    lanes = [slice(h * HG_HEAD_DIM, (h + 1) * HG_HEAD_DIM) for h in heads]
    o_intra = []
    for h in heads:
        a = lax.dot_general(qd[:, lanes[h]], ki16[:, lanes[h]], NT_DIMS,
                            preferred_element_type=F32)
        a = jnp.where(causal, a, 0.0).astype(BF16)
        o_intra.append(jnp.dot(a, vx[:, lanes[h]], preferred_element_type=F32))
    between(2)
    outs, new_states = [], []
    for h in heads:
        sl, st = lanes[h], states[h]
        inter = []
        for c in range(rows // HG_CHUNK):
            rs = slice(c * HG_CHUNK, (c + 1) * HG_CHUNK)
            inter.append(lax.dot_general(qd[rs, sl], st.astype(BF16), NT_DIMS,
                                         preferred_element_type=F32))
            dec = jnp.exp(b[(c + 1) * HG_CHUNK - 1:(c + 1) * HG_CHUNK, sl])
            k_to_end = (ki[rs, sl] * dec).astype(BF16)
            st = st * dec + lax.dot_general(vx[rs, sl], k_to_end, TN_DIMS,
                                            preferred_element_type=F32)
        new_states.append(st)
        o_h = o_intra[h] + jnp.concatenate(inter, axis=0)
        ms = jnp.mean(o_h * o_h, axis=-1, keepdims=True)
        outs.append(o_h * lax.rsqrt(ms + EPS))
    return jnp.concatenate(outs, axis=1), new_states


def _hgrn_layer_kernel(x_ref, g_ref, wq_ref, wf_ref, wi_ref, wg_ref, wcq_ref, wcg_ref,
                       lb_ref, ng_ref, k_ref, v_ref, bm_ref, bc_ref, h_ref, st_ref,
                       *, layer, sub, n_mix_steps, row_chunk):
    s = pl.program_id(1)
    j = pl.program_id(2)
    tm = x_ref.shape[0]
    n_sub = tm // sub
    heads = bm_ref.shape[1] // HG_HEAD_DIM

    def project(r, w_ref):
        return jnp.dot(h_ref[r * sub:(r + 1) * sub, :], w_ref[...], preferred_element_type=F32)

    @pl.when(j == 0)
    def _():
        _norm_rows_to(h_ref, x_ref, g_ref, row_chunk)

    @pl.when((j == 0) & (s == 0))
    def _():
        st_ref[...] = jnp.zeros_like(st_ref)

    @pl.when(j < n_mix_steps)
    def _():
        sm = jax.nn.softmax(lb_ref[...], axis=0)
        lb = jnp.sum(sm[:layer + 1], axis=0, keepdims=True) - sm[0:1]
        r_i = lax.broadcasted_iota(jnp.int32, (sub, sub), 0)
        c_i = lax.broadcasted_iota(jnp.int32, (sub, sub), 1)
        causal = (r_i // HG_CHUNK == c_i // HG_CHUNK) & (c_i <= r_i)
        ones_tril = jnp.where(causal, 1.0, 0.0).astype(BF16)
        states = [st_ref[j * heads + h] for h in range(heads)]
        w_refs = (wq_ref, wf_ref, wi_ref, wg_ref)
        cur = [project(0, w) for w in w_refs]
        for r in range(n_sub):
            nxt = [None] * len(w_refs)

            def between(k, r=r, nxt=nxt):
                if r + 1 < n_sub:
                    nxt[k] = project(r + 1, w_refs[k])

            qx, fx, ix, gx = cur
            o, states = _hgrn_rows(qx, fx, ix.astype(BF16), lb, states, causal, ones_tril, between)
            between(3)
            bm_ref[r * sub:(r + 1) * sub, :] = ((o * ng_ref[...]) * _silu(gx)).astype(BF16)
            cur = nxt
        for h in range(heads):
            st_ref[j * heads + h] = states[h]

    @pl.when(j >= n_mix_steps)
    def _():
        for r in range(n_sub):
            q = project(r, wcq_ref).astype(BF16)
            gate = project(r, wcg_ref)
            for hh in range(bc_ref.shape[1] // CA_HEAD_DIM):
                sl = slice(hh * CA_HEAD_DIM, (hh + 1) * CA_HEAD_DIM)
                o = _attend(q[:, sl], k_ref[:, sl], v_ref[:, sl])
                bc_ref[r * sub:(r + 1) * sub, sl] = (o * _silu(gate[:, sl])).astype(BF16)


def _hgrn_layer(x, g, w_in, lb_all, norm_g, kv, *, layer, batch, seq, n_mem, tm, sub):
    M, D = x.shape
    heads_per_step, ca_heads_per_step = 2, 2
    W = heads_per_step * HG_HEAD_DIM
    wc = ca_heads_per_step * CA_HEAD_DIM
    assert seq % tm == 0 and tm % sub == 0 and sub % HG_CHUNK == 0
    n_mix, n_ca = E_MIX // W, E_CA // wc
    spt = seq // tm
    mix_blk = lambda j: jnp.minimum(j, n_mix - 1)
    ca_blk = lambda j: jnp.clip(j - n_mix, 0, n_ca - 1)
    gate0 = 3 * E_MIX + E_CA
    assert gate0 % W == 0 and (3 * E_MIX) % wc == 0 and (gate0 + E_MIX) % wc == 0
    mix_w = lambda col0: pl.BlockSpec((D, W), lambda b, s, j: (0, col0 // W + mix_blk(j)))
    ca_w = lambda col0: pl.BlockSpec((D, wc), lambda b, s, j: (0, col0 // wc + ca_blk(j)))
    blocks = ([((tm, D), F32), ((1, D), F32)] + 4 * [((D, W), BF16)] + 2 * [((D, wc), BF16)]
              + [((DEPTH, W), F32), ((1, W), F32), ((n_mem, wc), BF16), ((n_mem, wc), BF16),
                 ((tm, W), BF16), ((tm, wc), BF16)])
    scratch = [((tm, D), BF16), ((E_MIX // HG_HEAD_DIM, HG_HEAD_DIM, HG_HEAD_DIM), F32)]
    return pl.pallas_call(
        functools.partial(_hgrn_layer_kernel, layer=layer, sub=sub, n_mix_steps=n_mix,
                          row_chunk=32),
        grid=(batch, spt, n_mix + n_ca),
        in_specs=[pl.BlockSpec((tm, D), lambda b, s, j: (b * spt + s, 0)),
                  pl.BlockSpec((1, D), lambda b, s, j: (0, 0)),
                  mix_w(0), mix_w(E_MIX), mix_w(2 * E_MIX), mix_w(gate0),
                  ca_w(3 * E_MIX), ca_w(gate0 + E_MIX),
                  pl.BlockSpec((DEPTH, W), lambda b, s, j: (0, mix_blk(j))),
                  pl.BlockSpec((1, W), lambda b, s, j: (0, mix_blk(j))),
                  pl.BlockSpec((n_mem, wc), lambda b, s, j: (b, ca_blk(j))),
                  pl.BlockSpec((n_mem, wc), lambda b, s, j: (b, E_CA // wc + ca_blk(j)))],
        out_specs=[pl.BlockSpec((tm, W), lambda b, s, j: (b * spt + s, mix_blk(j))),
                   pl.BlockSpec((tm, wc), lambda b, s, j: (b * spt + s, ca_blk(j)))],
        out_shape=[jax.ShapeDtypeStruct((M, E_MIX), BF16), jax.ShapeDtypeStruct((M, E_CA), BF16)],
        scratch_shapes=[pltpu.VMEM(s_, d_) for s_, d_ in scratch],
        compiler_params=_compiler_params(("parallel", "arbitrary", "arbitrary"), blocks, scratch,
                                         temps=32 * _nbytes((sub, W), F32)),
        name="hgrn_layer",
    )(x, g.reshape(1, D), w_in, w_in, w_in, w_in, w_in, w_in, lb_all,
      norm_g.reshape(1, E_MIX), kv, kv)


def _mem_attn_kernel(q_ref, kv_ref, gate_ref, o_ref):
    for h in range(CA_HEADS):
        sl = slice(h * CA_HEAD_DIM, (h + 1) * CA_HEAD_DIM)
        o = _attend(q_ref[:, sl], kv_ref[:, sl],
                    kv_ref[:, E_CA + h * CA_HEAD_DIM:E_CA + (h + 1) * CA_HEAD_DIM])
        o_ref[:, sl] = (o * _silu(gate_ref[:, sl])).astype(BF16)


def _mem_attention(q_ca, kv, gate, *, seq, n_mem, ts):
    M = q_ca.shape[0]
    assert seq % ts == 0 and E_MIX % E_CA == 0
    tiles_per_seq = seq // ts
    blocks = [((ts, E_CA), BF16), ((n_mem, 2 * E_CA), BF16), ((ts, E_CA), F32), ((ts, E_CA), BF16)]
    return pl.pallas_call(
        _mem_attn_kernel,
        grid=(M // ts,),
        in_specs=[pl.BlockSpec((ts, E_CA), lambda i: (i, 0)),
                  pl.BlockSpec((n_mem, 2 * E_CA), lambda i: (i // tiles_per_seq, 0)),
                  pl.BlockSpec((ts, E_CA), lambda i: (i, E_MIX // E_CA))],
        out_specs=pl.BlockSpec((ts, E_CA), lambda i: (i, 0)),
        out_shape=jax.ShapeDtypeStruct((M, E_CA), BF16),
        compiler_params=_compiler_params(("parallel",), blocks,
                                         temps=6 * _nbytes((ts, n_mem), F32)),
        name="mem_attention",
    )(q_ca, kv, gate)


def _out_proj_kernel(x_ref, bm_ref, bc_ref, wm_ref, wc_ref, o_ref):
    acc = jnp.dot(bm_ref[...], wm_ref[...], preferred_element_type=F32)
    acc = acc + jnp.dot(bc_ref[...], wc_ref[...], preferred_element_type=F32)
    o_ref[...] = x_ref[...] + acc


def _out_proj_norm_kernel(x_ref, bm_ref, bc_ref, wm_ref, wc_ref, g_ref, o_ref, *, row_chunk):
    j = pl.program_id(1)
    tn = x_ref.shape[1]
    n_col = o_ref.shape[1] // tn
    acc = jnp.dot(bm_ref[...], wm_ref[...], preferred_element_type=F32)
    acc = acc + jnp.dot(bc_ref[...], wc_ref[...], preferred_element_type=F32)
    y = x_ref[...] + acc
    for c in range(n_col):
        @pl.when(j == c)
        def _(c=c):
            o_ref[:, c * tn:(c + 1) * tn] = y

    @pl.when(j == n_col - 1)
    def _():
        def body(r, carry):
            rows = pl.ds(pl.multiple_of(r * row_chunk, row_chunk), row_chunk)
            o_ref[rows, :] = _rms_rows(o_ref[rows, :], g_ref[...])
            return carry
        lax.fori_loop(0, o_ref.shape[0] // row_chunk, body, 0)


def _out_proj(x, bm, bc, w_out, *, tm, tn, final_g=None):
    M, D = x.shape
    assert M % tm == 0 and D % tn == 0
    in_specs = [pl.BlockSpec((tm, tn), lambda i, j: (i, j)),
                pl.BlockSpec((tm, E_MIX), lambda i, j: (i, 0)),
                pl.BlockSpec((tm, E_CA), lambda i, j: (i, 0)),
                pl.BlockSpec((E_MIX, tn), lambda i, j: (0, j)),
                pl.BlockSpec((E_CA, tn), lambda i, j: (E_MIX // E_CA, j))]
    blocks = [((tm, tn), F32), ((tm, E_MIX), BF16), ((tm, E_CA), BF16),
              ((E_MIX, tn), BF16), ((E_CA, tn), BF16)]
    args = (x, bm, bc, w_out, w_out)
    if final_g is None:
        body, out_block, semantics = _out_proj_kernel, (tm, tn), ("parallel", "parallel")
        out_spec = pl.BlockSpec(out_block, lambda i, j: (i, j))
    else:
        body = functools.partial(_out_proj_norm_kernel, row_chunk=32)
        out_block, semantics = (tm, D), ("parallel", "arbitrary")
        out_spec = pl.BlockSpec(out_block, lambda i, j: (i, 0))
        in_specs.append(pl.BlockSpec((1, D), lambda i, j: (0, 0)))
        blocks.append(((1, D), F32))
        args += (final_g.reshape(1, D),)
    return pl.pallas_call(
        body,
        grid=(M // tm, D // tn),
        in_specs=in_specs,
        out_specs=out_spec,
        out_shape=jax.ShapeDtypeStruct((M, D), F32),
        compiler_params=_compiler_params(semantics, blocks + [(out_block, F32)],
                                         temps=2 * _nbytes((tm, tn), F32)),
        name="out_proj" if final_g is None else "out_proj_norm",
    )(*args)


def kernel(x, mem, norm_g, mem_norm_g, w_kv, w_out, pool_w_in, pool_w_grp, pool_scale,
           hgrn_w_in, hgrn_lb, hgrn_norm_g, final_g):
    batch, seq, d_model = x.shape
    n_mem = mem.shape[1]
    x2 = x.reshape(batch * seq, d_model)
    mem2 = mem.reshape(batch * n_mem, d_model)
    for i in range(DEPTH):
        j = i // N_MIXERS
        kv, = _norm_matmul(mem2, mem_norm_g, w_kv[i].astype(BF16), [(2 * E_CA, BF16)],
                           tm=512, tn=1024, name=f"kv_proj{i}")
        if i % N_MIXERS == 0:
            u, q_ca, gate = _norm_matmul(
                x2, norm_g[i], pool_w_in[j].astype(BF16),
                [(E_MIX, F32), (E_CA, BF16), (E_BRANCH, F32)], tm=1024, tn=512, name=f"in_proj{i}")
            bm = _pool_mixer(u, gate, pool_w_grp[j].astype(BF16), pool_scale[j], seq=seq, ts=256)
            bc = _mem_attention(q_ca, kv, gate, seq=seq, n_mem=n_mem, ts=512)
        else:
            bm, bc = _hgrn_layer(x2, norm_g[i], hgrn_w_in[j].astype(BF16), hgrn_lb,
                                 hgrn_norm_g[j], kv, layer=i, batch=batch, seq=seq, n_mem=n_mem,
                                 tm=1024, sub=256)
        x2 = _out_proj(x2, bm, bc, w_out[i].astype(BF16), tm=1024, tn=512,
                       final_g=final_g if i == DEPTH - 1 else None)
    return x2.reshape(batch, seq, d_model)
```

```python
import functools

import jax
import jax.numpy as jnp
from jax import lax
from jax.experimental import pallas as pl
from jax.experimental.pallas import tpu as pltpu

F32 = jnp.float32
BF16 = jnp.bfloat16

DEPTH = 2
N_MIXERS = 2
E_MIX = 3072
E_CA = 1024
E_BRANCH = E_MIX + E_CA
CA_HEADS = 4
CA_HEAD_DIM = E_CA // CA_HEADS
POOL_WINDOWS = (2, 4, 8, 16)
POOL_GROUP = E_MIX // len(POOL_WINDOWS)
POOL_HALO = max(POOL_WINDOWS)
HG_HEAD_DIM = 128
HG_CHUNK = 64
EPS = 1e-6

V7X_VMEM_BYTES = 64 * 1024 * 1024
NT_DIMS = (((1,), (1,)), ((), ()))
TN_DIMS = (((0,), (0,)), ((), ()))


def _nbytes(shape, dtype):
    n = 1
    for s in shape:
        n *= s
    return n * jnp.dtype(dtype).itemsize


def _compiler_params(semantics, blocks, scratch=(), temps=0):
    need = 2 * sum(_nbytes(s, d) for s, d in blocks) + sum(_nbytes(s, d) for s, d in scratch) + temps
    limit = min(need + need // 4 + (2 << 20), V7X_VMEM_BYTES - (4 << 20))
    return pltpu.CompilerParams(dimension_semantics=semantics, vmem_limit_bytes=int(limit))


def _silu(x):
    return x * jax.nn.sigmoid(x)


def _rms_rows(x, g):
    ms = jnp.mean(x * x, axis=-1, keepdims=True)
    return (x * lax.rsqrt(ms + EPS)) * g


def _norm_rows_to(h_ref, x_ref, g_ref, row_chunk):
    def body(r, carry):
        rows = pl.ds(pl.multiple_of(r * row_chunk, row_chunk), row_chunk)
        h_ref[rows, :] = _rms_rows(x_ref[rows, :], g_ref[...]).astype(BF16)
        return carry
    lax.fori_loop(0, x_ref.shape[0] // row_chunk, body, 0)


def _attend(q, k, v):
    s = lax.dot_general(q, k, NT_DIMS, preferred_element_type=F32) * (CA_HEAD_DIM ** -0.5)
    e = jnp.exp(s - jnp.max(s, axis=-1, keepdims=True))
    p = e / jnp.sum(e, axis=-1, keepdims=True)
    return jnp.dot(p.astype(BF16), v, preferred_element_type=F32)


def _norm_matmul_kernel(x_ref, g_ref, w_ref, *refs, seg_blocks, row_chunk):
    out_refs, h_ref = refs[:-1], refs[-1]
    j = pl.program_id(1)

    @pl.when(j == 0)
    def _():
        _norm_rows_to(h_ref, x_ref, g_ref, row_chunk)

    acc = jnp.dot(h_ref[...], w_ref[...], preferred_element_type=F32)
    lo = 0
    for o_ref, nb in zip(out_refs, seg_blocks):
        @pl.when((j >= lo) & (j < lo + nb))
        def _(o_ref=o_ref):
            o_ref[...] = acc.astype(o_ref.dtype)
        lo += nb


def _norm_matmul(x, g, w, segs, *, tm, tn, name):
    M, D = x.shape
    N = w.shape[1]
    assert M % tm == 0 and N % tn == 0 and sum(s[0] for s in segs) == N
    seg_blocks = []
    out_specs, out_shapes = [], []
    lo = 0
    for width, dtype in segs:
        assert width % tn == 0
        nb = width // tn
        out_specs.append(pl.BlockSpec(
            (tm, tn), lambda i, j, lo=lo, nb=nb: (i, jnp.clip(j - lo, 0, nb - 1))))
        out_shapes.append(jax.ShapeDtypeStruct((M, width), dtype))
        seg_blocks.append(nb)
        lo += nb
    blocks = [((tm, D), F32), ((1, D), F32), ((D, tn), BF16)] + [((tm, tn), d) for _, d in segs]
    return pl.pallas_call(
        functools.partial(_norm_matmul_kernel, seg_blocks=tuple(seg_blocks), row_chunk=32),
        grid=(M // tm, N // tn),
        in_specs=[pl.BlockSpec((tm, D), lambda i, j: (i, 0)),
                  pl.BlockSpec((1, D), lambda i, j: (0, 0)),
                  pl.BlockSpec((D, tn), lambda i, j: (0, j))],
        out_specs=out_specs,
        out_shape=out_shapes,
        scratch_shapes=[pltpu.VMEM((tm, D), BF16)],
        compiler_params=_compiler_params(("parallel", "arbitrary"), blocks,
                                         [((tm, D), BF16)], temps=_nbytes((tm, tn), F32)),
        name=name,
    )(x, g.reshape(1, D), w)


def _attention_step(project, wcq_ref, wcg_ref, k_ref, v_ref, bc_ref, n_sub, sub):
    for r in range(n_sub):
        q = project(r, wcq_ref).astype(BF16)
        gate = project(r, wcg_ref)
        for hh in range(bc_ref.shape[1] // CA_HEAD_DIM):
            sl = slice(hh * CA_HEAD_DIM, (hh + 1) * CA_HEAD_DIM)
            o = _attend(q[:, sl], k_ref[:, sl], v_ref[:, sl])
            bc_ref[r * sub:(r + 1) * sub, sl] = (o * _silu(gate[:, sl])).astype(BF16)


def _pool_rows(u, halo, window, pos0):
    s = jnp.concatenate([halo, u], axis=0)
    k = 1
    while k < POOL_HALO:
        s = jnp.where(k < window, s + pltpu.roll(s, k, axis=0), s)
        k *= 2
    pos = pos0 + lax.broadcasted_iota(jnp.int32, u.shape, 0)
    cnt = jnp.minimum(pos + 1, window).astype(F32)
    return s[POOL_HALO:, :] / cnt - u


def _pool_layer_kernel(x_ref, g_ref, wu_ref, wg_ref, wcq_ref, wcg_ref, wgrp_ref, scale_ref,
                       k_ref, v_ref, bm_ref, bc_ref, h_ref, halo_ref,
                       *, sub, n_mix_steps, row_chunk):
    s = pl.program_id(1)
    j = pl.program_id(2)
    tm = x_ref.shape[0]
    n_sub = tm // sub

    def project(r, w_ref):
        return jnp.dot(h_ref[r * sub:(r + 1) * sub, :], w_ref[...], preferred_element_type=F32)

    @pl.when(j == 0)
    def _():
        _norm_rows_to(h_ref, x_ref, g_ref, row_chunk)

    @pl.when(j < n_mix_steps)
    def _():
        window = jnp.int32(POOL_WINDOWS[0])
        for g in range(1, len(POOL_WINDOWS)):
            window = jnp.where(j == g, POOL_WINDOWS[g], window)
        halo = jnp.where(s == 0, 0.0, halo_ref[j])
        u, gate = project(0, wu_ref), project(0, wg_ref)
        for r in range(n_sub):
            pooled = _pool_rows(u, halo, window, s * tm + r * sub)
            halo = u[sub - POOL_HALO:, :]
            u_next = project(r + 1, wu_ref) if r + 1 < n_sub else None
            y = jnp.dot(pooled.astype(BF16), wgrp_ref[0], preferred_element_type=F32)
            gate_next = project(r + 1, wg_ref) if r + 1 < n_sub else None
            bm_ref[r * sub:(r + 1) * sub, :] = ((y * scale_ref[...]) * _silu(gate)).astype(BF16)
            u, gate = u_next, gate_next
        halo_ref[j] = halo

    @pl.when(j >= n_mix_steps)
    def _():
        _attention_step(project, wcq_ref, wcg_ref, k_ref, v_ref, bc_ref, n_sub, sub)


def _pool_layer(x, g, w_in, w_grp, scale, kv, *, batch, seq, n_mem, tm, sub):
    M, D = x.shape
    assert all(w & (w - 1) == 0 for w in POOL_WINDOWS) and seq % tm == 0 and tm % sub == 0
    ca_heads_per_step = 2
    C = POOL_GROUP
    wc = ca_heads_per_step * CA_HEAD_DIM
    n_mix, n_ca = len(POOL_WINDOWS), E_CA // wc
    spt = seq // tm
    gate0 = E_MIX + E_CA
    w_u = w_in[:, :E_MIX].astype(BF16)
    w_qc = w_in[:, E_MIX:gate0].astype(BF16)
    w_gm = w_in[:, gate0:gate0 + E_MIX].astype(BF16)
    w_gc = w_in[:, gate0 + E_MIX:].astype(BF16)
    mix_blk = lambda j: jnp.minimum(j, n_mix - 1)
    ca_blk = lambda j: jnp.clip(j - n_mix, 0, n_ca - 1)
    mix_w = pl.BlockSpec((D, C), lambda b, s, j: (0, mix_blk(j)))
    ca_w = pl.BlockSpec((D, wc), lambda b, s, j: (0, ca_blk(j)))
    blocks = ([((tm, D), F32), ((1, D), F32)] + 2 * [((D, C), BF16)] + 2 * [((D, wc), BF16)]
              + [((1, C, C), BF16), ((1, C), F32), ((n_mem, wc), BF16), ((n_mem, wc), BF16),
                 ((tm, C), BF16), ((tm, wc), BF16)])
    scratch = [((tm, D), BF16), ((n_mix, POOL_HALO, C), F32)]
    return pl.pallas_call(
        functools.partial(_pool_layer_kernel, sub=sub, n_mix_steps=n_mix, row_chunk=32),
        grid=(batch, spt, n_mix + n_ca),
        in_specs=[pl.BlockSpec((tm, D), lambda b, s, j: (b * spt + s, 0)),
                  pl.BlockSpec((1, D), lambda b, s, j: (0, 0)),
                  mix_w, mix_w, ca_w, ca_w,
                  pl.BlockSpec((1, C, C), lambda b, s, j: (mix_blk(j), 0, 0)),
                  pl.BlockSpec((1, C), lambda b, s, j: (0, mix_blk(j))),
                  pl.BlockSpec((n_mem, wc), lambda b, s, j: (b, ca_blk(j))),
                  pl.BlockSpec((n_mem, wc), lambda b, s, j: (b, E_CA // wc + ca_blk(j)))],
        out_specs=[pl.BlockSpec((tm, C), lambda b, s, j: (b * spt + s, mix_blk(j))),
                   pl.BlockSpec((tm, wc), lambda b, s, j: (b * spt + s, ca_blk(j)))],
        out_shape=[jax.ShapeDtypeStruct((M, E_MIX), BF16), jax.ShapeDtypeStruct((M, E_CA), BF16)],
        scratch_shapes=[pltpu.VMEM(s_, d_) for s_, d_ in scratch],
        compiler_params=_compiler_params(("parallel", "arbitrary", "arbitrary"), blocks, scratch,
                                         temps=12 * _nbytes((sub + POOL_HALO, C), F32)),
        name="pool_layer",
    )(x, g.reshape(1, D), w_u, w_gm, w_qc, w_gc, w_grp.astype(BF16), scale.reshape(1, E_MIX),
      kv, kv)


def _hgrn_rows(qx, fx, vx, lb, states, causal, ones_tril, between):
    rows = qx.shape[0]
    qs = _silu(qx) * (HG_HEAD_DIM ** -0.5)
    fg = lb + (1.0 - lb) * jax.nn.sigmoid(fx)
    kk = 1.0 - fg
    lf = jnp.log(fg)
    hi = lf.astype(BF16)
    r1 = lf - hi.astype(F32)
    mid = r1.astype(BF16)
    lo = (r1 - mid.astype(F32)).astype(BF16)
    between(0)
    b = (jnp.dot(ones_tril, hi, preferred_element_type=F32)
         + jnp.dot(ones_tril, mid, preferred_element_type=F32)
         + jnp.dot(ones_tril, lo, preferred_element_type=F32))
    qd = (qs * jnp.exp(b)).astype(BF16)
    ki = kk * jnp.exp(-b)
    ki16 = ki.astype(BF16)
    between(1)
    heads = range(len(states))
    lanes = [slice(h * HG_HEAD_DIM, (h + 1) * HG_HEAD_DIM) for h in heads]
    o_intra = []
    for h in heads:
        a = lax.dot_general(qd[:, lanes[h]], ki16[:, lanes[h]], NT_DIMS,
                            preferred_element_type=F32)
        a = jnp.where(causal, a, 0.0).astype(BF16)
        o_intra.append(jnp.dot(a, vx[:, lanes[h]], preferred_element_type=F32))
    between(2)
    outs, new_states = [], []
    for h in heads:
        sl, st = lanes[h], states[h]
        inter = []
        for c in range(rows // HG_CHUNK):
            rs = slice(c * HG_CHUNK, (c + 1) * HG_CHUNK)
            inter.append(lax.dot_general(qd[rs, sl], st.astype(BF16), NT_DIMS,
                                         preferred_element_type=F32))
            dec = jnp.exp(b[(c + 1) * HG_CHUNK - 1:(c + 1) * HG_CHUNK, sl])
            k_to_end = (ki[rs, sl] * dec).astype(BF16)
            st = st * dec + lax.dot_general(vx[rs, sl], k_to_end, TN_DIMS,
                                            preferred_element_type=F32)
        new_states.append(st)
        o_h = o_intra[h] + jnp.concatenate(inter, axis=0)
        ms = jnp.mean(o_h * o_h, axis=-1, keepdims=True)
        outs.append(o_h * lax.rsqrt(ms + EPS))
    return jnp.concatenate(outs, axis=1), new_states


def _hgrn_layer_kernel(x_ref, g_ref, wq_ref, wf_ref, wi_ref, wg_ref, wcq_ref, wcg_ref,
                       lb_ref, ng_ref, k_ref, v_ref, bm_ref, bc_ref, h_ref, st_ref,
                       *, layer, sub, n_mix_steps, row_chunk):
    s = pl.program_id(1)
    j = pl.program_id(2)
    tm = x_ref.shape[0]
    n_sub = tm // sub
    heads = bm_ref.shape[1] // HG_HEAD_DIM

    def project(r, w_ref):
        return jnp.dot(h_ref[r * sub:(r + 1) * sub, :], w_ref[...], preferred_element_type=F32)

    @pl.when(j == 0)
    def _():
        _norm_rows_to(h_ref, x_ref, g_ref, row_chunk)

    @pl.when((j == 0) & (s == 0))
    def _():
        st_ref[...] = jnp.zeros_like(st_ref)

    @pl.when(j < n_mix_steps)
    def _():
        sm = jax.nn.softmax(lb_ref[...], axis=0)
        lb = jnp.sum(sm[:layer + 1], axis=0, keepdims=True) - sm[0:1]
        r_i = lax.broadcasted_iota(jnp.int32, (sub, sub), 0)
        c_i = lax.broadcasted_iota(jnp.int32, (sub, sub), 1)
        causal = (r_i // HG_CHUNK == c_i // HG_CHUNK) & (c_i <= r_i)
        ones_tril = jnp.where(causal, 1.0, 0.0).astype(BF16)
        states = [st_ref[j * heads + h] for h in range(heads)]
        w_refs = (wq_ref, wf_ref, wi_ref, wg_ref)
        cur = [project(0, w) for w in w_refs]
        for r in range(n_sub):
            nxt = [None] * len(w_refs)

            def between(k, r=r, nxt=nxt):
                if r + 1 < n_sub:
                    nxt[k] = project(r + 1, w_refs[k])

            qx, fx, ix, gx = cur
            o, states = _hgrn_rows(qx, fx, ix.astype(BF16), lb, states, causal, ones_tril, between)
            between(3)
            bm_ref[r * sub:(r + 1) * sub, :] = ((o * ng_ref[...]) * _silu(gx)).astype(BF16)
            cur = nxt
        for h in range(heads):
            st_ref[j * heads + h] = states[h]

    @pl.when(j >= n_mix_steps)
    def _():
        _attention_step(project, wcq_ref, wcg_ref, k_ref, v_ref, bc_ref, n_sub, sub)


def _hgrn_layer(x, g, w_in, lb_all, norm_g, kv, *, layer, batch, seq, n_mem, tm, sub):
    M, D = x.shape
    heads_per_step, ca_heads_per_step = 2, 2
    W = heads_per_step * HG_HEAD_DIM
    wc = ca_heads_per_step * CA_HEAD_DIM
    assert seq % tm == 0 and tm % sub == 0 and sub % HG_CHUNK == 0
    n_mix, n_ca = E_MIX // W, E_CA // wc
    spt = seq // tm
    mix_blk = lambda j: jnp.minimum(j, n_mix - 1)
    ca_blk = lambda j: jnp.clip(j - n_mix, 0, n_ca - 1)
    gate0 = 3 * E_MIX + E_CA
    assert gate0 % W == 0 and (3 * E_MIX) % wc == 0 and (gate0 + E_MIX) % wc == 0
    mix_w = lambda col0: pl.BlockSpec((D, W), lambda b, s, j: (0, col0 // W + mix_blk(j)))
    ca_w = lambda col0: pl.BlockSpec((D, wc), lambda b, s, j: (0, col0 // wc + ca_blk(j)))
    blocks = ([((tm, D), F32), ((1, D), F32)] + 4 * [((D, W), BF16)] + 2 * [((D, wc), BF16)]
              + [((DEPTH, W), F32), ((1, W), F32), ((n_mem, wc), BF16), ((n_mem, wc), BF16),
                 ((tm, W), BF16), ((tm, wc), BF16)])
    scratch = [((tm, D), BF16), ((E_MIX // HG_HEAD_DIM, HG_HEAD_DIM, HG_HEAD_DIM), F32)]
    return pl.pallas_call(
        functools.partial(_hgrn_layer_kernel, layer=layer, sub=sub, n_mix_steps=n_mix,
                          row_chunk=32),
        grid=(batch, spt, n_mix + n_ca),
        in_specs=[pl.BlockSpec((tm, D), lambda b, s, j: (b * spt + s, 0)),
                  pl.BlockSpec((1, D), lambda b, s, j: (0, 0)),
                  mix_w(0), mix_w(E_MIX), mix_w(2 * E_MIX), mix_w(gate0),
                  ca_w(3 * E_MIX), ca_w(gate0 + E_MIX),
                  pl.BlockSpec((DEPTH, W), lambda b, s, j: (0, mix_blk(j))),
                  pl.BlockSpec((1, W), lambda b, s, j: (0, mix_blk(j))),
                  pl.BlockSpec((n_mem, wc), lambda b, s, j: (b, ca_blk(j))),
                  pl.BlockSpec((n_mem, wc), lambda b, s, j: (b, E_CA // wc + ca_blk(j)))],
        out_specs=[pl.BlockSpec((tm, W), lambda b, s, j: (b * spt + s, mix_blk(j))),
                   pl.BlockSpec((tm, wc), lambda b, s, j: (b * spt + s, ca_blk(j)))],
        out_shape=[jax.ShapeDtypeStruct((M, E_MIX), BF16), jax.ShapeDtypeStruct((M, E_CA), BF16)],
        scratch_shapes=[pltpu.VMEM(s_, d_) for s_, d_ in scratch],
        compiler_params=_compiler_params(("parallel", "arbitrary", "arbitrary"), blocks, scratch,
                                         temps=32 * _nbytes((sub, W), F32)),
        name="hgrn_layer",
    )(x, g.reshape(1, D), w_in, w_in, w_in, w_in, w_in, w_in, lb_all,
      norm_g.reshape(1, E_MIX), kv, kv)


def _out_proj_kernel(x_ref, bm_ref, bc_ref, wm_ref, wc_ref, o_ref):
    acc = jnp.dot(bm_ref[...], wm_ref[...], preferred_element_type=F32)
    acc = acc + jnp.dot(bc_ref[...], wc_ref[...], preferred_element_type=F32)
    o_ref[...] = x_ref[...] + acc


def _out_proj_norm_kernel(x_ref, bm_ref, bc_ref, wm_ref, wc_ref, g_ref, o_ref, *, row_chunk):
    j = pl.program_id(1)
    tn = x_ref.shape[1]
    n_col = o_ref.shape[1] // tn
    acc = jnp.dot(bm_ref[...], wm_ref[...], preferred_element_type=F32)
    acc = acc + jnp.dot(bc_ref[...], wc_ref[...], preferred_element_type=F32)
    y = x_ref[...] + acc
    for c in range(n_col):
        @pl.when(j == c)
        def _(c=c):
            o_ref[:, c * tn:(c + 1) * tn] = y

    @pl.when(j == n_col - 1)
    def _():
        def body(r, carry):
            rows = pl.ds(pl.multiple_of(r * row_chunk, row_chunk), row_chunk)
            o_ref[rows, :] = _rms_rows(o_ref[rows, :], g_ref[...])
            return carry
        lax.fori_loop(0, o_ref.shape[0] // row_chunk, body, 0)


def _out_proj(x, bm, bc, w_out, *, tm, tn, final_g=None):
    M, D = x.shape
    assert M % tm == 0 and D % tn == 0
    in_specs = [pl.BlockSpec((tm, tn), lambda i, j: (i, j)),
                pl.BlockSpec((tm, E_MIX), lambda i, j: (i, 0)),
                pl.BlockSpec((tm, E_CA), lambda i, j: (i, 0)),
                pl.BlockSpec((E_MIX, tn), lambda i, j: (0, j)),
                pl.BlockSpec((E_CA, tn), lambda i, j: (E_MIX // E_CA, j))]
    blocks = [((tm, tn), F32), ((tm, E_MIX), BF16), ((tm, E_CA), BF16),
              ((E_MIX, tn), BF16), ((E_CA, tn), BF16)]
    args = (x, bm, bc, w_out, w_out)
    if final_g is None:
        body, out_block, semantics = _out_proj_kernel, (tm, tn), ("parallel", "parallel")
        out_spec = pl.BlockSpec(out_block, lambda i, j: (i, j))
    else:
        body = functools.partial(_out_proj_norm_kernel, row_chunk=32)
        out_block, semantics = (tm, D), ("parallel", "arbitrary")
        out_spec = pl.BlockSpec(out_block, lambda i, j: (i, 0))
        in_specs.append(pl.BlockSpec((1, D), lambda i, j: (0, 0)))
        blocks.append(((1, D), F32))
        args += (final_g.reshape(1, D),)
    return pl.pallas_call(
        body,
        grid=(M // tm, D // tn),
        in_specs=in_specs,
        out_specs=out_spec,
        out_shape=jax.ShapeDtypeStruct((M, D), F32),
        compiler_params=_compiler_params(semantics, blocks + [(out_block, F32)],
                                         temps=2 * _nbytes((tm, tn), F32)),
        name="out_proj" if final_g is None else "out_proj_norm",
    )(*args)


def kernel(x, mem, norm_g, mem_norm_g, w_kv, w_out, pool_w_in, pool_w_grp, pool_scale,
           hgrn_w_in, hgrn_lb, hgrn_norm_g, final_g):
    batch, seq, d_model = x.shape
    n_mem = mem.shape[1]
    x2 = x.reshape(batch * seq, d_model)
    mem2 = mem.reshape(batch * n_mem, d_model)
    for i in range(DEPTH):
        j = i // N_MIXERS
        kv, = _norm_matmul(mem2, mem_norm_g, w_kv[i].astype(BF16), [(2 * E_CA, BF16)],
                           tm=512, tn=1024, name=f"kv_proj{i}")
        if i % N_MIXERS == 0:
            bm, bc = _pool_layer(x2, norm_g[i], pool_w_in[j], pool_w_grp[j], pool_scale[j], kv,
                                 batch=batch, seq=seq, n_mem=n_mem, tm=1024, sub=256)
        else:
            bm, bc = _hgrn_layer(x2, norm_g[i], hgrn_w_in[j].astype(BF16), hgrn_lb,
                                 hgrn_norm_g[j], kv, layer=i, batch=batch, seq=seq, n_mem=n_mem,
                                 tm=1024, sub=256)
        x2 = _out_proj(x2, bm, bc, w_out[i].astype(BF16), tm=1024, tn=512,
                       final_g=final_g if i == DEPTH - 1 else None)
    return x2.reshape(batch, seq, d_model)
```

```python
import functools

import jax
import jax.numpy as jnp
from jax import lax
from jax.experimental import pallas as pl
from jax.experimental.pallas import tpu as pltpu

F32 = jnp.float32
BF16 = jnp.bfloat16

DEPTH = 2
N_MIXERS = 2
E_MIX = 3072
E_CA = 1024
E_BRANCH = E_MIX + E_CA
CA_HEADS = 4
CA_HEAD_DIM = E_CA // CA_HEADS
POOL_WINDOWS = (2, 4, 8, 16)
POOL_GROUP = E_MIX // len(POOL_WINDOWS)
POOL_HALO = max(POOL_WINDOWS)
HG_HEAD_DIM = 128
HG_CHUNK = 64
HG_SPAN = 2 * HG_CHUNK
EPS = 1e-6

V7X_VMEM_BYTES = 64 * 1024 * 1024
NT_DIMS = (((1,), (1,)), ((), ()))
TN_DIMS = (((0,), (0,)), ((), ()))


def _nbytes(shape, dtype):
    n = 1
    for s in shape:
        n *= s
    return n * jnp.dtype(dtype).itemsize


def _compiler_params(semantics, blocks, scratch=(), temps=0):
    need = 2 * sum(_nbytes(s, d) for s, d in blocks) + sum(_nbytes(s, d) for s, d in scratch) + temps
    limit = min(need + need // 4 + (2 << 20), V7X_VMEM_BYTES - (4 << 20))
    return pltpu.CompilerParams(dimension_semantics=semantics, vmem_limit_bytes=int(limit))


def _silu(x):
    return x * jax.nn.sigmoid(x)


def _rms_rows(x, g):
    ms = jnp.mean(x * x, axis=-1, keepdims=True)
    return (x * lax.rsqrt(ms + EPS)) * g


def _norm_rows_to(h_ref, x_ref, g_ref, row_chunk):
    def body(r, carry):
        rows = pl.ds(pl.multiple_of(r * row_chunk, row_chunk), row_chunk)
        h_ref[rows, :] = _rms_rows(x_ref[rows, :], g_ref[...]).astype(BF16)
        return carry
    lax.fori_loop(0, x_ref.shape[0] // row_chunk, body, 0)


def _attend(q, k, v):
    s = lax.dot_general(q, k, NT_DIMS, preferred_element_type=F32) * (CA_HEAD_DIM ** -0.5)
    e = jnp.exp(s - jnp.max(s, axis=-1, keepdims=True))
    p = e / jnp.sum(e, axis=-1, keepdims=True)
    return jnp.dot(p.astype(BF16), v, preferred_element_type=F32)


def _norm_matmul_kernel(x_ref, g_ref, w_ref, *refs, seg_blocks, row_chunk):
    out_refs, h_ref = refs[:-1], refs[-1]
    j = pl.program_id(1)

    @pl.when(j == 0)
    def _():
        _norm_rows_to(h_ref, x_ref, g_ref, row_chunk)

    acc = jnp.dot(h_ref[...], w_ref[...], preferred_element_type=F32)
    lo = 0
    for o_ref, nb in zip(out_refs, seg_blocks):
        @pl.when((j >= lo) & (j < lo + nb))
        def _(o_ref=o_ref):
            o_ref[...] = acc.astype(o_ref.dtype)
        lo += nb


def _norm_matmul(x, g, w, segs, *, tm, tn, name):
    M, D = x.shape
    N = w.shape[1]
    assert M % tm == 0 and N % tn == 0 and sum(s[0] for s in segs) == N
    seg_blocks = []
    out_specs, out_shapes = [], []
    lo = 0
    for width, dtype in segs:
        assert width % tn == 0
        nb = width // tn
        out_specs.append(pl.BlockSpec(
            (tm, tn), lambda i, j, lo=lo, nb=nb: (i, jnp.clip(j - lo, 0, nb - 1))))
        out_shapes.append(jax.ShapeDtypeStruct((M, width), dtype))
        seg_blocks.append(nb)
        lo += nb
    blocks = [((tm, D), F32), ((1, D), F32), ((D, tn), BF16)] + [((tm, tn), d) for _, d in segs]
    return pl.pallas_call(
        functools.partial(_norm_matmul_kernel, seg_blocks=tuple(seg_blocks), row_chunk=32),
        grid=(M // tm, N // tn),
        in_specs=[pl.BlockSpec((tm, D), lambda i, j: (i, 0)),
                  pl.BlockSpec((1, D), lambda i, j: (0, 0)),
                  pl.BlockSpec((D, tn), lambda i, j: (0, j))],
        out_specs=out_specs,
        out_shape=out_shapes,
        scratch_shapes=[pltpu.VMEM((tm, D), BF16)],
        compiler_params=_compiler_params(("parallel", "arbitrary"), blocks,
                                         [((tm, D), BF16)], temps=_nbytes((tm, tn), F32)),
        name=name,
    )(x, g.reshape(1, D), w)


def _attention_step(project, wcq_ref, wcg_ref, k_ref, v_ref, bc_ref, n_sub, sub):
    for r in range(n_sub):
        q = project(r, wcq_ref).astype(BF16)
        gate = project(r, wcg_ref)
        for hh in range(bc_ref.shape[1] // CA_HEAD_DIM):
            sl = slice(hh * CA_HEAD_DIM, (hh + 1) * CA_HEAD_DIM)
            o = _attend(q[:, sl], k_ref[:, sl], v_ref[:, sl])
            bc_ref[r * sub:(r + 1) * sub, sl] = (o * _silu(gate[:, sl])).astype(BF16)


def _pool_rows(u, halo, window, pos0):
    s = jnp.concatenate([halo, u], axis=0)
    k = 1
    while k < POOL_HALO:
        s = jnp.where(k < window, s + pltpu.roll(s, k, axis=0), s)
        k *= 2
    pos = pos0 + lax.broadcasted_iota(jnp.int32, u.shape, 0)
    cnt = jnp.minimum(pos + 1, window).astype(F32)
    return s[POOL_HALO:, :] / cnt - u


def _pool_layer_kernel(x_ref, g_ref, wu_ref, wg_ref, wcq_ref, wcg_ref, wgrp_ref, scale_ref,
                       k_ref, v_ref, bm_ref, bc_ref, h_ref, halo_ref,
                       *, sub, n_mix_steps, row_chunk):
    s = pl.program_id(1)
    j = pl.program_id(2)
    tm = x_ref.shape[0]
    n_sub = tm // sub

    def project(r, w_ref):
        return jnp.dot(h_ref[r * sub:(r + 1) * sub, :], w_ref[...], preferred_element_type=F32)

    @pl.when(j == 0)
    def _():
        _norm_rows_to(h_ref, x_ref, g_ref, row_chunk)

    @pl.when(j < n_mix_steps)
    def _():
        window = jnp.int32(POOL_WINDOWS[0])
        for g in range(1, len(POOL_WINDOWS)):
            window = jnp.where(j == g, POOL_WINDOWS[g], window)
        halo = jnp.where(s == 0, 0.0, halo_ref[j])
        u, gate = project(0, wu_ref), project(0, wg_ref)
        for r in range(n_sub):
            pooled = _pool_rows(u, halo, window, s * tm + r * sub)
            halo = u[sub - POOL_HALO:, :]
            u_next = project(r + 1, wu_ref) if r + 1 < n_sub else None
            y = jnp.dot(pooled.astype(BF16), wgrp_ref[0], preferred_element_type=F32)
            gate_next = project(r + 1, wg_ref) if r + 1 < n_sub else None
            bm_ref[r * sub:(r + 1) * sub, :] = ((y * scale_ref[...]) * _silu(gate)).astype(BF16)
            u, gate = u_next, gate_next
        halo_ref[j] = halo

    @pl.when(j >= n_mix_steps)
    def _():
        _attention_step(project, wcq_ref, wcg_ref, k_ref, v_ref, bc_ref, n_sub, sub)


def _pool_layer(x, g, w_in, w_grp, scale, kv, *, batch, seq, n_mem, tm, sub):
    M, D = x.shape
    assert all(w & (w - 1) == 0 for w in POOL_WINDOWS) and seq % tm == 0 and tm % sub == 0
    ca_heads_per_step = 2
    C = POOL_GROUP
    wc = ca_heads_per_step * CA_HEAD_DIM
    n_mix, n_ca = len(POOL_WINDOWS), E_CA // wc
    spt = seq // tm
    gate0 = E_MIX + E_CA
    w_u = w_in[:, :E_MIX].astype(BF16)
    w_qc = w_in[:, E_MIX:gate0].astype(BF16)
    w_gm = w_in[:, gate0:gate0 + E_MIX].astype(BF16)
    w_gc = w_in[:, gate0 + E_MIX:].astype(BF16)
    mix_blk = lambda j: jnp.minimum(j, n_mix - 1)
    ca_blk = lambda j: jnp.clip(j - n_mix, 0, n_ca - 1)
    mix_w = pl.BlockSpec((D, C), lambda b, s, j: (0, mix_blk(j)))
    ca_w = pl.BlockSpec((D, wc), lambda b, s, j: (0, ca_blk(j)))
    blocks = ([((tm, D), F32), ((1, D), F32)] + 2 * [((D, C), BF16)] + 2 * [((D, wc), BF16)]
              + [((1, C, C), BF16), ((1, C), F32), ((n_mem, wc), BF16), ((n_mem, wc), BF16),
                 ((tm, C), BF16), ((tm, wc), BF16)])
    scratch = [((tm, D), BF16), ((n_mix, POOL_HALO, C), F32)]
    return pl.pallas_call(
        functools.partial(_pool_layer_kernel, sub=sub, n_mix_steps=n_mix, row_chunk=32),
        grid=(batch, spt, n_mix + n_ca),
        in_specs=[pl.BlockSpec((tm, D), lambda b, s, j: (b * spt + s, 0)),
                  pl.BlockSpec((1, D), lambda b, s, j: (0, 0)),
                  mix_w, mix_w, ca_w, ca_w,
                  pl.BlockSpec((1, C, C), lambda b, s, j: (mix_blk(j), 0, 0)),
                  pl.BlockSpec((1, C), lambda b, s, j: (0, mix_blk(j))),
                  pl.BlockSpec((n_mem, wc), lambda b, s, j: (b, ca_blk(j))),
                  pl.BlockSpec((n_mem, wc), lambda b, s, j: (b, E_CA // wc + ca_blk(j)))],
        out_specs=[pl.BlockSpec((tm, C), lambda b, s, j: (b * spt + s, mix_blk(j))),
                   pl.BlockSpec((tm, wc), lambda b, s, j: (b * spt + s, ca_blk(j)))],
        out_shape=[jax.ShapeDtypeStruct((M, E_MIX), BF16), jax.ShapeDtypeStruct((M, E_CA), BF16)],
        scratch_shapes=[pltpu.VMEM(s_, d_) for s_, d_ in scratch],
        compiler_params=_compiler_params(("parallel", "arbitrary", "arbitrary"), blocks, scratch,
                                         temps=12 * _nbytes((sub + POOL_HALO, C), F32)),
        name="pool_layer",
    )(x, g.reshape(1, D), w_u, w_gm, w_qc, w_gc, w_grp.astype(BF16), scale.reshape(1, E_MIX),
      kv, kv)


def _hgrn_rows(qx, fx, vx, lb, states, causal, ones_tril, between):
    rows = qx.shape[0]
    assert rows == 2 * HG_SPAN
    qs = _silu(qx) * (HG_HEAD_DIM ** -0.5)
    fg = lb + (1.0 - lb) * jax.nn.sigmoid(fx)
    kk = 1.0 - fg
    lf = jnp.log(fg)
    hi = lf.astype(BF16)
    r1 = lf - hi.astype(F32)
    mid = r1.astype(BF16)
    lo = (r1 - mid.astype(F32)).astype(BF16)
    between(0)
    g = (jnp.dot(ones_tril, hi, preferred_element_type=F32)
         + jnp.dot(ones_tril, mid, preferred_element_type=F32)
         + jnp.dot(ones_tril, lo, preferred_element_type=F32))
    sp, ch = HG_SPAN, HG_CHUNK
    g = [g[:sp], g[sp:] - g[sp - 1:sp]]
    cat = lambda parts: jnp.concatenate(parts, axis=0)
    e_in = jnp.exp(cat(g))
    e_out = jnp.exp(cat([gp[sp - 1:] - gp for gp in g]))
    d_q = jnp.exp(cat([gp - gp[ch - 1:ch] for gp in g]))
    d_k = jnp.exp(cat([gp[ch - 1:ch] - gp for gp in g]))
    dec0, dec1 = e_in[sp - 1:sp], e_in[rows - 1:rows]
    q_in, k_out = qs * e_in, kk * e_out
    q_dg, k_dg = (qs * d_q).astype(BF16), (kk * d_k).astype(BF16)
    q_off, k_off = q_in[sp:].astype(BF16), k_out[:sp].astype(BF16)
    q_st = cat([q_in[:sp], q_in[sp:] * dec0]).astype(BF16)
    k_st = cat([k_out[:sp] * dec1, k_out[sp:]]).astype(BF16)
    dec = dec0 * dec1
    between(1)
    heads = range(len(states))
    lanes = [slice(h * HG_HEAD_DIM, (h + 1) * HG_HEAD_DIM) for h in heads]
    attn = []
    for h in heads:
        sl = lanes[h]
        a_dg = lax.dot_general(q_dg[:, sl], k_dg[:, sl], NT_DIMS, preferred_element_type=F32)
        a_dg = jnp.where(causal, a_dg, 0.0)
        a_off = lax.dot_general(q_off[:, sl], k_off[:, sl], NT_DIMS, preferred_element_type=F32)
        attn.append(cat([a_dg[:sp], jnp.concatenate([a_off, a_dg[sp:, sp:]], axis=1)]).astype(BF16))
    between(2)
    outs, new_states = [], []
    for h in heads:
        sl, st = lanes[h], states[h]
        o_h = (jnp.dot(attn[h], vx[:, sl], preferred_element_type=F32)
               + lax.dot_general(q_st[:, sl], st.astype(BF16), NT_DIMS, preferred_element_type=F32))
        new_states.append(st * dec[:, sl] + lax.dot_general(vx[:, sl], k_st[:, sl], TN_DIMS,
                                                            preferred_element_type=F32))
        ms = jnp.mean(o_h * o_h, axis=-1, keepdims=True)
        outs.append(o_h * lax.rsqrt(ms + EPS))
    return jnp.concatenate(outs, axis=1), new_states


def _hgrn_layer_kernel(x_ref, g_ref, wq_ref, wf_ref, wi_ref, wg_ref, wcq_ref, wcg_ref,
                       lb_ref, ng_ref, k_ref, v_ref, bm_ref, bc_ref, h_ref, st_ref,
                       *, layer, sub, n_mix_steps, row_chunk):
    s = pl.program_id(1)
    j = pl.program_id(2)
    tm = x_ref.shape[0]
    n_sub = tm // sub
    heads = bm_ref.shape[1] // HG_HEAD_DIM

    def project(r, w_ref):
        return jnp.dot(h_ref[r * sub:(r + 1) * sub, :], w_ref[...], preferred_element_type=F32)

    @pl.when(j == 0)
    def _():
        _norm_rows_to(h_ref, x_ref, g_ref, row_chunk)

    @pl.when((j == 0) & (s == 0))
    def _():
        st_ref[...] = jnp.zeros_like(st_ref)

    @pl.when(j < n_mix_steps)
    def _():
        sm = jax.nn.softmax(lb_ref[...], axis=0)
        lb = jnp.sum(sm[:layer + 1], axis=0, keepdims=True) - sm[0:1]
        r_i = lax.broadcasted_iota(jnp.int32, (sub, sub), 0)
        c_i = lax.broadcasted_iota(jnp.int32, (sub, sub), 1)
        causal = (r_i // HG_SPAN == c_i // HG_SPAN) & (c_i <= r_i)
        ones_tril = jnp.where(c_i <= r_i, 1.0, 0.0).astype(BF16)
        states = [st_ref[j * heads + h] for h in range(heads)]
        w_refs = (wq_ref, wf_ref, wi_ref, wg_ref)
        cur = [project(0, w) for w in w_refs]
        for r in range(n_sub):
            nxt = [None] * len(w_refs)

            def between(k, r=r, nxt=nxt):
                if r + 1 < n_sub:
                    nxt[k] = project(r + 1, w_refs[k])

            qx, fx, ix, gx = cur
            o, states = _hgrn_rows(qx, fx, ix.astype(BF16), lb, states, causal, ones_tril, between)
            between(3)
            bm_ref[r * sub:(r + 1) * sub, :] = ((o * ng_ref[...]) * _silu(gx)).astype(BF16)
            cur = nxt
        for h in range(heads):
            st_ref[j * heads + h] = states[h]

    @pl.when(j >= n_mix_steps)
    def _():
        _attention_step(project, wcq_ref, wcg_ref, k_ref, v_ref, bc_ref, n_sub, sub)


def _hgrn_layer(x, g, w_in, lb_all, norm_g, kv, *, layer, batch, seq, n_mem, tm, sub):
    M, D = x.shape
    heads_per_step, ca_heads_per_step = 2, 2
    W = heads_per_step * HG_HEAD_DIM
    wc = ca_heads_per_step * CA_HEAD_DIM
    assert seq % tm == 0 and tm % sub == 0 and sub == 2 * HG_SPAN
    n_mix, n_ca = E_MIX // W, E_CA // wc
    spt = seq // tm
    mix_blk = lambda j: jnp.minimum(j, n_mix - 1)
    ca_blk = lambda j: jnp.clip(j - n_mix, 0, n_ca - 1)
    gate0 = 3 * E_MIX + E_CA
    assert gate0 % W == 0 and (3 * E_MIX) % wc == 0 and (gate0 + E_MIX) % wc == 0
    mix_w = lambda col0: pl.BlockSpec((D, W), lambda b, s, j: (0, col0 // W + mix_blk(j)))
    ca_w = lambda col0: pl.BlockSpec((D, wc), lambda b, s, j: (0, col0 // wc + ca_blk(j)))
    blocks = ([((tm, D), F32), ((1, D), F32)] + 4 * [((D, W), BF16)] + 2 * [((D, wc), BF16)]
              + [((DEPTH, W), F32), ((1, W), F32), ((n_mem, wc), BF16), ((n_mem, wc), BF16),
                 ((tm, W), BF16), ((tm, wc), BF16)])
    scratch = [((tm, D), BF16), ((E_MIX // HG_HEAD_DIM, HG_HEAD_DIM, HG_HEAD_DIM), F32)]
    return pl.pallas_call(
        functools.partial(_hgrn_layer_kernel, layer=layer, sub=sub, n_mix_steps=n_mix,
                          row_chunk=32),
        grid=(batch, spt, n_mix + n_ca),
        in_specs=[pl.BlockSpec((tm, D), lambda b, s, j: (b * spt + s, 0)),
                  pl.BlockSpec((1, D), lambda b, s, j: (0, 0)),
                  mix_w(0), mix_w(E_MIX), mix_w(2 * E_MIX), mix_w(gate0),
                  ca_w(3 * E_MIX), ca_w(gate0 + E_MIX),
                  pl.BlockSpec((DEPTH, W), lambda b, s, j: (0, mix_blk(j))),
                  pl.BlockSpec((1, W), lambda b, s, j: (0, mix_blk(j))),
                  pl.BlockSpec((n_mem, wc), lambda b, s, j: (b, ca_blk(j))),
                  pl.BlockSpec((n_mem, wc), lambda b, s, j: (b, E_CA // wc + ca_blk(j)))],
        out_specs=[pl.BlockSpec((tm, W), lambda b, s, j: (b * spt + s, mix_blk(j))),
                   pl.BlockSpec((tm, wc), lambda b, s, j: (b * spt + s, ca_blk(j)))],
        out_shape=[jax.ShapeDtypeStruct((M, E_MIX), BF16), jax.ShapeDtypeStruct((M, E_CA), BF16)],
        scratch_shapes=[pltpu.VMEM(s_, d_) for s_, d_ in scratch],
        compiler_params=_compiler_params(("parallel", "arbitrary", "arbitrary"), blocks, scratch,
                                         temps=32 * _nbytes((sub, W), F32)),
        name="hgrn_layer",
    )(x, g.reshape(1, D), w_in, w_in, w_in, w_in, w_in, w_in, lb_all,
      norm_g.reshape(1, E_MIX), kv, kv)


def _out_proj_kernel(x_ref, bm_ref, bc_ref, wm_ref, wc_ref, o_ref):
    acc = jnp.dot(bm_ref[...], wm_ref[...], preferred_element_type=F32)
    acc = acc + jnp.dot(bc_ref[...], wc_ref[...], preferred_element_type=F32)
    o_ref[...] = x_ref[...] + acc


def _out_proj_norm_kernel(x_ref, bm_ref, bc_ref, wm_ref, wc_ref, g_ref, o_ref, *, row_chunk):
    j = pl.program_id(1)
    tn = x_ref.shape[1]
    n_col = o_ref.shape[1] // tn
    acc = jnp.dot(bm_ref[...], wm_ref[...], preferred_element_type=F32)
    acc = acc + jnp.dot(bc_ref[...], wc_ref[...], preferred_element_type=F32)
    y = x_ref[...] + acc
    for c in range(n_col):
        @pl.when(j == c)
        def _(c=c):
            o_ref[:, c * tn:(c + 1) * tn] = y

    @pl.when(j == n_col - 1)
    def _():
        def body(r, carry):
            rows = pl.ds(pl.multiple_of(r * row_chunk, row_chunk), row_chunk)
            o_ref[rows, :] = _rms_rows(o_ref[rows, :], g_ref[...])
            return carry
        lax.fori_loop(0, o_ref.shape[0] // row_chunk, body, 0)


def _out_proj(x, bm, bc, w_out, *, tm, tn, final_g=None):
    M, D = x.shape
    assert M % tm == 0 and D % tn == 0
    in_specs = [pl.BlockSpec((tm, tn), lambda i, j: (i, j)),
                pl.BlockSpec((tm, E_MIX), lambda i, j: (i, 0)),
                pl.BlockSpec((tm, E_CA), lambda i, j: (i, 0)),
                pl.BlockSpec((E_MIX, tn), lambda i, j: (0, j)),
                pl.BlockSpec((E_CA, tn), lambda i, j: (E_MIX // E_CA, j))]
    blocks = [((tm, tn), F32), ((tm, E_MIX), BF16), ((tm, E_CA), BF16),
              ((E_MIX, tn), BF16), ((E_CA, tn), BF16)]
    args = (x, bm, bc, w_out, w_out)
    if final_g is None:
        body, out_block, semantics = _out_proj_kernel, (tm, tn), ("parallel", "parallel")
        out_spec = pl.BlockSpec(out_block, lambda i, j: (i, j))
    else:
        body = functools.partial(_out_proj_norm_kernel, row_chunk=32)
        out_block, semantics = (tm, D), ("parallel", "arbitrary")
        out_spec = pl.BlockSpec(out_block, lambda i, j: (i, 0))
        in_specs.append(pl.BlockSpec((1, D), lambda i, j: (0, 0)))
        blocks.append(((1, D), F32))
        args += (final_g.reshape(1, D),)
    return pl.pallas_call(
        body,
        grid=(M // tm, D // tn),
        in_specs=in_specs,
        out_specs=out_spec,
        out_shape=jax.ShapeDtypeStruct((M, D), F32),
        compiler_params=_compiler_params(semantics, blocks + [(out_block, F32)],
                                         temps=2 * _nbytes((tm, tn), F32)),
        name="out_proj" if final_g is None else "out_proj_norm",
    )(*args)


def kernel(x, mem, norm_g, mem_norm_g, w_kv, w_out, pool_w_in, pool_w_grp, pool_scale,
           hgrn_w_in, hgrn_lb, hgrn_norm_g, final_g):
    batch, seq, d_model = x.shape
    n_mem = mem.shape[1]
    x2 = x.reshape(batch * seq, d_model)
    mem2 = mem.reshape(batch * n_mem, d_model)
    for i in range(DEPTH):
        j = i // N_MIXERS
        kv, = _norm_matmul(mem2, mem_norm_g, w_kv[i].astype(BF16), [(2 * E_CA, BF16)],
                           tm=512, tn=1024, name=f"kv_proj{i}")
        if i % N_MIXERS == 0:
            bm, bc = _pool_layer(x2, norm_g[i], pool_w_in[j], pool_w_grp[j], pool_scale[j], kv,
                                 batch=batch, seq=seq, n_mem=n_mem, tm=1024, sub=256)
        else:
            bm, bc = _hgrn_layer(x2, norm_g[i], hgrn_w_in[j].astype(BF16), hgrn_lb,
                                 hgrn_norm_g[j], kv, layer=i, batch=batch, seq=seq, n_mem=n_mem,
                                 tm=1024, sub=256)
        x2 = _out_proj(x2, bm, bc, w_out[i].astype(BF16), tm=1024, tn=512,
                       final_g=final_g if i == DEPTH - 1 else None)
    return x2.reshape(batch, seq, d_model)
```

```python
import functools

import jax
import jax.numpy as jnp
from jax import lax
from jax.experimental import pallas as pl
from jax.experimental.pallas import tpu as pltpu

F32 = jnp.float32
BF16 = jnp.bfloat16

DEPTH = 2
N_MIXERS = 2
E_MIX = 3072
E_CA = 1024
E_BRANCH = E_MIX + E_CA
CA_HEADS = 4
CA_HEAD_DIM = E_CA // CA_HEADS
POOL_WINDOWS = (2, 4, 8, 16)
POOL_GROUP = E_MIX // len(POOL_WINDOWS)
POOL_HALO = max(POOL_WINDOWS)
HG_HEAD_DIM = 128
HG_CHUNK = 64
HG_SPAN = 2 * HG_CHUNK
EPS = 1e-6

V7X_VMEM_BYTES = 64 * 1024 * 1024
F32_SUBLANES = 8
NT_DIMS = (((1,), (1,)), ((), ()))
TN_DIMS = (((0,), (0,)), ((), ()))


def _nbytes(shape, dtype):
    n = 1
    for s in shape:
        n *= s
    return n * jnp.dtype(dtype).itemsize


def _compiler_params(semantics, blocks, scratch=(), temps=0):
    need = 2 * sum(_nbytes(s, d) for s, d in blocks) + sum(_nbytes(s, d) for s, d in scratch) + temps
    limit = min(need + need // 4 + (2 << 20), V7X_VMEM_BYTES - (4 << 20))
    return pltpu.CompilerParams(dimension_semantics=semantics, vmem_limit_bytes=int(limit))


def _silu(x):
    return x * jax.nn.sigmoid(x)


def _rms_rows(x, g):
    ms = jnp.mean(x * x, axis=-1, keepdims=True)
    return (x * lax.rsqrt(ms + EPS)) * g


def _norm_rows_to(h_ref, x_ref, g_ref, row_chunk):
    def body(r, carry):
        rows = pl.ds(pl.multiple_of(r * row_chunk, row_chunk), row_chunk)
        h_ref[rows, :] = _rms_rows(x_ref[rows, :], g_ref[...]).astype(BF16)
        return carry
    lax.fori_loop(0, x_ref.shape[0] // row_chunk, body, 0)


def _attend(q, k, v):
    s = lax.dot_general(q, k, NT_DIMS, preferred_element_type=F32) * (CA_HEAD_DIM ** -0.5)
    e = jnp.exp(s - jnp.max(s, axis=-1, keepdims=True))
    p = e / jnp.sum(e, axis=-1, keepdims=True)
    return jnp.dot(p.astype(BF16), v, preferred_element_type=F32)


def _norm_matmul_kernel(x_ref, g_ref, w_ref, *refs, seg_blocks, row_chunk):
    out_refs, h_ref = refs[:-1], refs[-1]
    j = pl.program_id(1)

    @pl.when(j == 0)
    def _():
        _norm_rows_to(h_ref, x_ref, g_ref, row_chunk)

    acc = jnp.dot(h_ref[...], w_ref[...], preferred_element_type=F32)
    lo = 0
    for o_ref, nb in zip(out_refs, seg_blocks):
        @pl.when((j >= lo) & (j < lo + nb))
        def _(o_ref=o_ref):
            o_ref[...] = acc.astype(o_ref.dtype)
        lo += nb


def _norm_matmul(x, g, w, segs, *, tm, tn, name):
    M, D = x.shape
    N = w.shape[1]
    assert M % tm == 0 and N % tn == 0 and sum(s[0] for s in segs) == N
    seg_blocks = []
    out_specs, out_shapes = [], []
    lo = 0
    for width, dtype in segs:
        assert width % tn == 0
        nb = width // tn
        out_specs.append(pl.BlockSpec(
            (tm, tn), lambda i, j, lo=lo, nb=nb: (i, jnp.clip(j - lo, 0, nb - 1))))
        out_shapes.append(jax.ShapeDtypeStruct((M, width), dtype))
        seg_blocks.append(nb)
        lo += nb
    blocks = [((tm, D), F32), ((1, D), F32), ((D, tn), BF16)] + [((tm, tn), d) for _, d in segs]
    return pl.pallas_call(
        functools.partial(_norm_matmul_kernel, seg_blocks=tuple(seg_blocks), row_chunk=32),
        grid=(M // tm, N // tn),
        in_specs=[pl.BlockSpec((tm, D), lambda i, j: (i, 0)),
                  pl.BlockSpec((1, D), lambda i, j: (0, 0)),
                  pl.BlockSpec((D, tn), lambda i, j: (0, j))],
        out_specs=out_specs,
        out_shape=out_shapes,
        scratch_shapes=[pltpu.VMEM((tm, D), BF16)],
        compiler_params=_compiler_params(("parallel", "arbitrary"), blocks,
                                         [((tm, D), BF16)], temps=_nbytes((tm, tn), F32)),
        name=name,
    )(x, g.reshape(1, D), w)


def _attention_step(project, wcq_ref, wcg_ref, k_ref, v_ref, bc_ref, n_sub, sub):
    for r in range(n_sub):
        q = project(r, wcq_ref).astype(BF16)
        gate = project(r, wcg_ref)
        for hh in range(bc_ref.shape[1] // CA_HEAD_DIM):
            sl = slice(hh * CA_HEAD_DIM, (hh + 1) * CA_HEAD_DIM)
            o = _attend(q[:, sl], k_ref[:, sl], v_ref[:, sl])
            bc_ref[r * sub:(r + 1) * sub, sl] = (o * _silu(gate[:, sl])).astype(BF16)


def _pool_rows(u, halo, window, pos0):
    s = jnp.concatenate([halo, u], axis=0)
    k = 1
    while k < POOL_HALO:
        s = jnp.where(k < window, s + pltpu.roll(s, k, axis=0), s)
        k *= 2
    pos = pos0 + lax.broadcasted_iota(jnp.int32, u.shape, 0)
    cnt = jnp.minimum(pos + 1, window).astype(F32)
    return s[POOL_HALO:, :] / cnt - u


def _pool_layer_kernel(x_ref, g_ref, wu_ref, wg_ref, wcq_ref, wcg_ref, wgrp_ref, scale_ref,
                       k_ref, v_ref, bm_ref, bc_ref, h_ref, halo_ref,
                       *, sub, n_mix_steps, row_chunk):
    s = pl.program_id(1)
    j = pl.program_id(2)
    tm = x_ref.shape[0]
    n_sub = tm // sub

    def project(r, w_ref):
        return jnp.dot(h_ref[r * sub:(r + 1) * sub, :], w_ref[...], preferred_element_type=F32)

    @pl.when(j == 0)
    def _():
        _norm_rows_to(h_ref, x_ref, g_ref, row_chunk)

    @pl.when(j < n_mix_steps)
    def _():
        window = jnp.int32(POOL_WINDOWS[0])
        for g in range(1, len(POOL_WINDOWS)):
            window = jnp.where(j == g, POOL_WINDOWS[g], window)
        halo = jnp.where(s == 0, 0.0, halo_ref[j])
        u, gate = project(0, wu_ref), project(0, wg_ref)
        for r in range(n_sub):
            pooled = _pool_rows(u, halo, window, s * tm + r * sub)
            halo = u[sub - POOL_HALO:, :]
            u_next = project(r + 1, wu_ref) if r + 1 < n_sub else None
            y = jnp.dot(pooled.astype(BF16), wgrp_ref[0], preferred_element_type=F32)
            gate_next = project(r + 1, wg_ref) if r + 1 < n_sub else None
            bm_ref[r * sub:(r + 1) * sub, :] = ((y * scale_ref[...]) * _silu(gate)).astype(BF16)
            u, gate = u_next, gate_next
        halo_ref[j] = halo

    @pl.when(j >= n_mix_steps)
    def _():
        _attention_step(project, wcq_ref, wcg_ref, k_ref, v_ref, bc_ref, n_sub, sub)


def _pool_layer(x, g, w_in, w_grp, scale, kv, *, batch, seq, n_mem, tm, sub):
    M, D = x.shape
    assert all(w & (w - 1) == 0 for w in POOL_WINDOWS) and seq % tm == 0 and tm % sub == 0
    ca_heads_per_step = 2
    C = POOL_GROUP
    wc = ca_heads_per_step * CA_HEAD_DIM
    n_mix, n_ca = len(POOL_WINDOWS), E_CA // wc
    spt = seq // tm
    gate0 = E_MIX + E_CA
    w_u = w_in[:, :E_MIX].astype(BF16)
    w_qc = w_in[:, E_MIX:gate0].astype(BF16)
    w_gm = w_in[:, gate0:gate0 + E_MIX].astype(BF16)
    w_gc = w_in[:, gate0 + E_MIX:].astype(BF16)
    mix_blk = lambda j: jnp.minimum(j, n_mix - 1)
    ca_blk = lambda j: jnp.clip(j - n_mix, 0, n_ca - 1)
    mix_w = pl.BlockSpec((D, C), lambda b, s, j: (0, mix_blk(j)))
    ca_w = pl.BlockSpec((D, wc), lambda b, s, j: (0, ca_blk(j)))
    blocks = ([((tm, D), F32), ((1, D), F32)] + 2 * [((D, C), BF16)] + 2 * [((D, wc), BF16)]
              + [((1, C, C), BF16), ((1, C), F32), ((n_mem, wc), BF16), ((n_mem, wc), BF16),
                 ((tm, C), BF16), ((tm, wc), BF16)])
    scratch = [((tm, D), BF16), ((n_mix, POOL_HALO, C), F32)]
    return pl.pallas_call(
        functools.partial(_pool_layer_kernel, sub=sub, n_mix_steps=n_mix, row_chunk=32),
        grid=(batch, spt, n_mix + n_ca),
        in_specs=[pl.BlockSpec((tm, D), lambda b, s, j: (b * spt + s, 0)),
                  pl.BlockSpec((1, D), lambda b, s, j: (0, 0)),
                  mix_w, mix_w, ca_w, ca_w,
                  pl.BlockSpec((1, C, C), lambda b, s, j: (mix_blk(j), 0, 0)),
                  pl.BlockSpec((1, C), lambda b, s, j: (0, mix_blk(j))),
                  pl.BlockSpec((n_mem, wc), lambda b, s, j: (b, ca_blk(j))),
                  pl.BlockSpec((n_mem, wc), lambda b, s, j: (b, E_CA // wc + ca_blk(j)))],
        out_specs=[pl.BlockSpec((tm, C), lambda b, s, j: (b * spt + s, mix_blk(j))),
                   pl.BlockSpec((tm, wc), lambda b, s, j: (b * spt + s, ca_blk(j)))],
        out_shape=[jax.ShapeDtypeStruct((M, E_MIX), BF16), jax.ShapeDtypeStruct((M, E_CA), BF16)],
        scratch_shapes=[pltpu.VMEM(s_, d_) for s_, d_ in scratch],
        compiler_params=_compiler_params(("parallel", "arbitrary", "arbitrary"), blocks, scratch,
                                         temps=12 * _nbytes((sub + POOL_HALO, C), F32)),
        name="pool_layer",
    )(x, g.reshape(1, D), w_u, w_gm, w_qc, w_gc, w_grp.astype(BF16), scale.reshape(1, E_MIX),
      kv, kv)


def _cumsum_rows(x):
    rows, width = x.shape
    r = lax.broadcasted_iota(jnp.int32, (F32_SUBLANES, width), 0)
    k = 1
    while k < F32_SUBLANES:
        sh = pltpu.roll(x, k, axis=0)
        top = jnp.where(r >= k, sh[:F32_SUBLANES], 0.0)
        x = x + jnp.concatenate([top, sh[F32_SUBLANES:]], axis=0)
        k *= 2
    while k < rows:
        x = jnp.concatenate([x[:k], x[k:] + x[:rows - k]], axis=0)
        k *= 2
    return x


def _hgrn_rows(qx, fx, vx, lb, states, causal, between):
    rows = qx.shape[0]
    assert rows == 2 * HG_SPAN
    qs = _silu(qx) * (HG_HEAD_DIM ** -0.5)
    fg = lb + (1.0 - lb) * jax.nn.sigmoid(fx)
    kk = 1.0 - fg
    lf = jnp.log(fg)
    between(0)
    sp, ch = HG_SPAN, HG_CHUNK
    g = [_cumsum_rows(lf[:sp]), _cumsum_rows(lf[sp:])]
    cat = lambda parts: jnp.concatenate(parts, axis=0)
    e_in = jnp.exp(cat(g))
    e_out = jnp.exp(cat([gp[sp - 1:] - gp for gp in g]))
    d_q = jnp.exp(cat([gp - gp[ch - 1:ch] for gp in g]))
    d_k = jnp.exp(cat([gp[ch - 1:ch] - gp for gp in g]))
    dec0, dec1 = e_in[sp - 1:sp], e_in[rows - 1:rows]
    q_in, k_out = qs * e_in, kk * e_out
    q_dg, k_dg = (qs * d_q).astype(BF16), (kk * d_k).astype(BF16)
    q_off, k_off = q_in[sp:].astype(BF16), k_out[:sp].astype(BF16)
    q_st = cat([q_in[:sp], q_in[sp:] * dec0]).astype(BF16)
    k_st = cat([k_out[:sp] * dec1, k_out[sp:]]).astype(BF16)
    dec = dec0 * dec1
    between(1)
    heads = range(len(states))
    lanes = [slice(h * HG_HEAD_DIM, (h + 1) * HG_HEAD_DIM) for h in heads]
    attn = []
    for h in heads:
        sl = lanes[h]
        a_dg = lax.dot_general(q_dg[:, sl], k_dg[:, sl], NT_DIMS, preferred_element_type=F32)
        a_dg = jnp.where(causal, a_dg, 0.0)
        a_off = lax.dot_general(q_off[:, sl], k_off[:, sl], NT_DIMS, preferred_element_type=F32)
        attn.append(cat([a_dg[:sp], jnp.concatenate([a_off, a_dg[sp:, sp:]], axis=1)]).astype(BF16))
    between(2)
    outs, new_states = [], []
    for h in heads:
        sl, st = lanes[h], states[h]
        o_h = (jnp.dot(attn[h], vx[:, sl], preferred_element_type=F32)
               + lax.dot_general(q_st[:, sl], st.astype(BF16), NT_DIMS, preferred_element_type=F32))
        new_states.append(st * dec[:, sl] + lax.dot_general(vx[:, sl], k_st[:, sl], TN_DIMS,
                                                            preferred_element_type=F32))
        ms = jnp.mean(o_h * o_h, axis=-1, keepdims=True)
        outs.append(o_h * lax.rsqrt(ms + EPS))
    return jnp.concatenate(outs, axis=1), new_states


def _hgrn_layer_kernel(x_ref, g_ref, wq_ref, wf_ref, wi_ref, wg_ref, wcq_ref, wcg_ref,
                       lb_ref, ng_ref, k_ref, v_ref, bm_ref, bc_ref, h_ref, st_ref,
                       *, layer, sub, n_mix_steps, row_chunk):
    s = pl.program_id(1)
    j = pl.program_id(2)
    tm = x_ref.shape[0]
    n_sub = tm // sub
    heads = bm_ref.shape[1] // HG_HEAD_DIM

    def project(r, w_ref):
        return jnp.dot(h_ref[r * sub:(r + 1) * sub, :], w_ref[...], preferred_element_type=F32)

    @pl.when(j == 0)
    def _():
        _norm_rows_to(h_ref, x_ref, g_ref, row_chunk)

    @pl.when((j == 0) & (s == 0))
    def _():
        st_ref[...] = jnp.zeros_like(st_ref)

    @pl.when(j < n_mix_steps)
    def _():
        sm = jax.nn.softmax(lb_ref[...], axis=0)
        lb = jnp.sum(sm[:layer + 1], axis=0, keepdims=True) - sm[0:1]
        r_i = lax.broadcasted_iota(jnp.int32, (sub, sub), 0)
        c_i = lax.broadcasted_iota(jnp.int32, (sub, sub), 1)
        causal = (r_i // HG_SPAN == c_i // HG_SPAN) & (c_i <= r_i)
        states = [st_ref[j * heads + h] for h in range(heads)]
        w_refs = (wq_ref, wf_ref, wi_ref, wg_ref)
        cur = [project(0, w) for w in w_refs]
        for r in range(n_sub):
            nxt = [None] * len(w_refs)

            def between(k, r=r, nxt=nxt):
                if r + 1 < n_sub:
                    nxt[k] = project(r + 1, w_refs[k])

            qx, fx, ix, gx = cur
            o, states = _hgrn_rows(qx, fx, ix.astype(BF16), lb, states, causal, between)
            between(3)
            bm_ref[r * sub:(r + 1) * sub, :] = ((o * ng_ref[...]) * _silu(gx)).astype(BF16)
            cur = nxt
        for h in range(heads):
            st_ref[j * heads + h] = states[h]

    @pl.when(j >= n_mix_steps)
    def _():
        _attention_step(project, wcq_ref, wcg_ref, k_ref, v_ref, bc_ref, n_sub, sub)


def _hgrn_layer(x, g, w_in, lb_all, norm_g, kv, *, layer, batch, seq, n_mem, tm, sub):
    M, D = x.shape
    heads_per_step, ca_heads_per_step = 2, 2
    W = heads_per_step * HG_HEAD_DIM
    wc = ca_heads_per_step * CA_HEAD_DIM
    assert seq % tm == 0 and tm % sub == 0 and sub == 2 * HG_SPAN
    n_mix, n_ca = E_MIX // W, E_CA // wc
    spt = seq // tm
    mix_blk = lambda j: jnp.minimum(j, n_mix - 1)
    ca_blk = lambda j: jnp.clip(j - n_mix, 0, n_ca - 1)
    gate0 = 3 * E_MIX + E_CA
    assert gate0 % W == 0 and (3 * E_MIX) % wc == 0 and (gate0 + E_MIX) % wc == 0
    mix_w = lambda col0: pl.BlockSpec((D, W), lambda b, s, j: (0, col0 // W + mix_blk(j)))
    ca_w = lambda col0: pl.BlockSpec((D, wc), lambda b, s, j: (0, col0 // wc + ca_blk(j)))
    blocks = ([((tm, D), F32), ((1, D), F32)] + 4 * [((D, W), BF16)] + 2 * [((D, wc), BF16)]
              + [((DEPTH, W), F32), ((1, W), F32), ((n_mem, wc), BF16), ((n_mem, wc), BF16),
                 ((tm, W), BF16), ((tm, wc), BF16)])
    scratch = [((tm, D), BF16), ((E_MIX // HG_HEAD_DIM, HG_HEAD_DIM, HG_HEAD_DIM), F32)]
    return pl.pallas_call(
        functools.partial(_hgrn_layer_kernel, layer=layer, sub=sub, n_mix_steps=n_mix,
                          row_chunk=32),
        grid=(batch, spt, n_mix + n_ca),
        in_specs=[pl.BlockSpec((tm, D), lambda b, s, j: (b * spt + s, 0)),
                  pl.BlockSpec((1, D), lambda b, s, j: (0, 0)),
                  mix_w(0), mix_w(E_MIX), mix_w(2 * E_MIX), mix_w(gate0),
                  ca_w(3 * E_MIX), ca_w(gate0 + E_MIX),
                  pl.BlockSpec((DEPTH, W), lambda b, s, j: (0, mix_blk(j))),
                  pl.BlockSpec((1, W), lambda b, s, j: (0, mix_blk(j))),
                  pl.BlockSpec((n_mem, wc), lambda b, s, j: (b, ca_blk(j))),
                  pl.BlockSpec((n_mem, wc), lambda b, s, j: (b, E_CA // wc + ca_blk(j)))],
        out_specs=[pl.BlockSpec((tm, W), lambda b, s, j: (b * spt + s, mix_blk(j))),
                   pl.BlockSpec((tm, wc), lambda b, s, j: (b * spt + s, ca_blk(j)))],
        out_shape=[jax.ShapeDtypeStruct((M, E_MIX), BF16), jax.ShapeDtypeStruct((M, E_CA), BF16)],
        scratch_shapes=[pltpu.VMEM(s_, d_) for s_, d_ in scratch],
        compiler_params=_compiler_params(("parallel", "arbitrary", "arbitrary"), blocks, scratch,
                                         temps=32 * _nbytes((sub, W), F32)),
        name="hgrn_layer",
    )(x, g.reshape(1, D), w_in, w_in, w_in, w_in, w_in, w_in, lb_all,
      norm_g.reshape(1, E_MIX), kv, kv)


def _out_proj_kernel(x_ref, bm_ref, bc_ref, wm_ref, wc_ref, *rest):
    acc = jnp.dot(bm_ref[...], wm_ref[...], preferred_element_type=F32)
    acc = acc + jnp.dot(bc_ref[...], wc_ref[...], preferred_element_type=F32)
    y = x_ref[...] + acc
    if len(rest) == 1:
        o_ref, = rest
        o_ref[...] = y
    else:
        g_ref, o_ref = rest
        o_ref[...] = _rms_rows(y, g_ref[...])


def _out_proj(x, bm, bc, w_out, *, tm, final_g=None):
    M, D = x.shape
    assert M % tm == 0
    resident = pl.Buffered(1)
    in_specs = [pl.BlockSpec((tm, D), lambda i: (i, 0)),
                pl.BlockSpec((tm, E_MIX), lambda i: (i, 0)),
                pl.BlockSpec((tm, E_CA), lambda i: (i, 0)),
                pl.BlockSpec((E_MIX, D), lambda i: (0, 0), pipeline_mode=resident),
                pl.BlockSpec((E_CA, D), lambda i: (E_MIX // E_CA, 0), pipeline_mode=resident)]
    blocks = [((tm, D), F32), ((tm, E_MIX), BF16), ((tm, E_CA), BF16), ((tm, D), F32)]
    single = [((E_MIX, D), BF16), ((E_CA, D), BF16)]
    args = (x, bm, bc, w_out, w_out)
    if final_g is not None:
        in_specs.append(pl.BlockSpec((1, D), lambda i: (0, 0)))
        blocks.append(((1, D), F32))
        args += (final_g.reshape(1, D),)
    return pl.pallas_call(
        _out_proj_kernel,
        grid=(M // tm,),
        in_specs=in_specs,
        out_specs=pl.BlockSpec((tm, D), lambda i: (i, 0)),
        out_shape=jax.ShapeDtypeStruct((M, D), F32),
        compiler_params=_compiler_params(("parallel",), blocks, single,
                                         temps=2 * _nbytes((tm, D), F32)),
        name="out_proj" if final_g is None else "out_proj_norm",
    )(*args)


def kernel(x, mem, norm_g, mem_norm_g, w_kv, w_out, pool_w_in, pool_w_grp, pool_scale,
           hgrn_w_in, hgrn_lb, hgrn_norm_g, final_g):
    batch, seq, d_model = x.shape
    n_mem = mem.shape[1]
    x2 = x.reshape(batch * seq, d_model)
    mem2 = mem.reshape(batch * n_mem, d_model)
    for i in range(DEPTH):
        j = i // N_MIXERS
        kv, = _norm_matmul(mem2, mem_norm_g, w_kv[i].astype(BF16), [(2 * E_CA, BF16)],
                           tm=512, tn=1024, name=f"kv_proj{i}")
        if i % N_MIXERS == 0:
            bm, bc = _pool_layer(x2, norm_g[i], pool_w_in[j], pool_w_grp[j], pool_scale[j], kv,
                                 batch=batch, seq=seq, n_mem=n_mem, tm=1024, sub=256)
        else:
            bm, bc = _hgrn_layer(x2, norm_g[i], hgrn_w_in[j].astype(BF16), hgrn_lb,
                                 hgrn_norm_g[j], kv, layer=i, batch=batch, seq=seq, n_mem=n_mem,
                                 tm=1024, sub=256)
        x2 = _out_proj(x2, bm, bc, w_out[i].astype(BF16), tm=512,
                       final_g=final_g if i == DEPTH - 1 else None)
    return x2.reshape(batch, seq, d_model)
```

```python
import functools

import jax
import jax.numpy as jnp
from jax import lax
from jax.experimental import pallas as pl
from jax.experimental.pallas import tpu as pltpu

F32 = jnp.float32
BF16 = jnp.bfloat16

DEPTH = 2
N_MIXERS = 2
E_MIX = 3072
E_CA = 1024
E_BRANCH = E_MIX + E_CA
CA_HEADS = 4
CA_HEAD_DIM = E_CA // CA_HEADS
POOL_WINDOWS = (2, 4, 8, 16)
POOL_GROUP = E_MIX // len(POOL_WINDOWS)
POOL_HALO = max(POOL_WINDOWS)
HG_HEAD_DIM = 128
HG_CHUNK = 64
HG_SPAN = 2 * HG_CHUNK
EPS = 1e-6

V7X_VMEM_BYTES = 64 * 1024 * 1024
F32_SUBLANES = 8
NT_DIMS = (((1,), (1,)), ((), ()))
TN_DIMS = (((0,), (0,)), ((), ()))


def _nbytes(shape, dtype):
    n = 1
    for s in shape:
        n *= s
    return n * jnp.dtype(dtype).itemsize


def _compiler_params(semantics, blocks, scratch=(), temps=0):
    need = 2 * sum(_nbytes(s, d) for s, d in blocks) + sum(_nbytes(s, d) for s, d in scratch) + temps
    limit = min(need + need // 4 + (2 << 20), V7X_VMEM_BYTES - (4 << 20))
    return pltpu.CompilerParams(dimension_semantics=semantics, vmem_limit_bytes=int(limit))


def _silu(x):
    return x * jax.nn.sigmoid(x)


def _rms_rows(x, g):
    ms = jnp.mean(x * x, axis=-1, keepdims=True)
    return (x * lax.rsqrt(ms + EPS)) * g


def _norm_rows_to(h_ref, x_ref, g_ref, row_chunk):
    def body(r, carry):
        rows = pl.ds(pl.multiple_of(r * row_chunk, row_chunk), row_chunk)
        h_ref[rows, :] = _rms_rows(x_ref[rows, :], g_ref[...]).astype(BF16)
        return carry
    lax.fori_loop(0, x_ref.shape[0] // row_chunk, body, 0)


def _attend(q, k, v):
    s = lax.dot_general(q, k, NT_DIMS, preferred_element_type=F32) * (CA_HEAD_DIM ** -0.5)
    e = jnp.exp(s - jnp.max(s, axis=-1, keepdims=True))
    p = e / jnp.sum(e, axis=-1, keepdims=True)
    return jnp.dot(p.astype(BF16), v, preferred_element_type=F32)


def _norm_matmul_kernel(x_ref, g_ref, w_ref, *refs, seg_blocks, row_chunk):
    out_refs, h_ref = refs[:-1], refs[-1]
    j = pl.program_id(1)

    @pl.when(j == 0)
    def _():
        _norm_rows_to(h_ref, x_ref, g_ref, row_chunk)

    acc = jnp.dot(h_ref[...], w_ref[...], preferred_element_type=F32)
    lo = 0
    for o_ref, nb in zip(out_refs, seg_blocks):
        @pl.when((j >= lo) & (j < lo + nb))
        def _(o_ref=o_ref):
            o_ref[...] = acc.astype(o_ref.dtype)
        lo += nb


def _norm_matmul(x, g, w, segs, *, tm, tn, name):
    M, D = x.shape
    N = w.shape[1]
    assert M % tm == 0 and N % tn == 0 and sum(s[0] for s in segs) == N
    seg_blocks = []
    out_specs, out_shapes = [], []
    lo = 0
    for width, dtype in segs:
        assert width % tn == 0
        nb = width // tn
        out_specs.append(pl.BlockSpec(
            (tm, tn), lambda i, j, lo=lo, nb=nb: (i, jnp.clip(j - lo, 0, nb - 1))))
        out_shapes.append(jax.ShapeDtypeStruct((M, width), dtype))
        seg_blocks.append(nb)
        lo += nb
    blocks = [((tm, D), F32), ((1, D), F32), ((D, tn), BF16)] + [((tm, tn), d) for _, d in segs]
    return pl.pallas_call(
        functools.partial(_norm_matmul_kernel, seg_blocks=tuple(seg_blocks), row_chunk=32),
        grid=(M // tm, N // tn),
        in_specs=[pl.BlockSpec((tm, D), lambda i, j: (i, 0)),
                  pl.BlockSpec((1, D), lambda i, j: (0, 0)),
                  pl.BlockSpec((D, tn), lambda i, j: (0, j))],
        out_specs=out_specs,
        out_shape=out_shapes,
        scratch_shapes=[pltpu.VMEM((tm, D), BF16)],
        compiler_params=_compiler_params(("parallel", "arbitrary"), blocks,
                                         [((tm, D), BF16)], temps=_nbytes((tm, tn), F32)),
        name=name,
    )(x, g.reshape(1, D), w)


def _attention_step(project, wcq_ref, wcg_ref, k_ref, v_ref, bc_ref, n_sub, sub):
    for r in range(n_sub):
        q = project(r, wcq_ref).astype(BF16)
        gate = project(r, wcg_ref)
        for hh in range(bc_ref.shape[1] // CA_HEAD_DIM):
            sl = slice(hh * CA_HEAD_DIM, (hh + 1) * CA_HEAD_DIM)
            o = _attend(q[:, sl], k_ref[:, sl], v_ref[:, sl])
            bc_ref[r * sub:(r + 1) * sub, sl] = (o * _silu(gate[:, sl])).astype(BF16)


def _pool_rows(u, halo, window, pos0):
    s = jnp.concatenate([halo, u], axis=0)
    k = 1
    while k < POOL_HALO:
        s = jnp.where(k < window, s + pltpu.roll(s, k, axis=0), s)
        k *= 2
    pos = pos0 + lax.broadcasted_iota(jnp.int32, u.shape, 0)
    cnt = jnp.minimum(pos + 1, window).astype(F32)
    return s[POOL_HALO:, :] / cnt - u


def _pool_layer_kernel(x_ref, g_ref, wu_ref, wg_ref, wcq_ref, wcg_ref, wgrp_ref, scale_ref,
                       k_ref, v_ref, bm_ref, bc_ref, h_ref, halo_ref,
                       *, sub, n_mix_steps, row_chunk):
    s = pl.program_id(1)
    j = pl.program_id(2)
    tm = x_ref.shape[0]
    n_sub = tm // sub

    def project(r, w_ref):
        return jnp.dot(h_ref[r * sub:(r + 1) * sub, :], w_ref[...], preferred_element_type=F32)

    @pl.when(j == 0)
    def _():
        _norm_rows_to(h_ref, x_ref, g_ref, row_chunk)

    @pl.when(j < n_mix_steps)
    def _():
        window = jnp.int32(POOL_WINDOWS[0])
        for g in range(1, len(POOL_WINDOWS)):
            window = jnp.where(j == g, POOL_WINDOWS[g], window)
        halo = jnp.where(s == 0, 0.0, halo_ref[j])
        u, gate = project(0, wu_ref), project(0, wg_ref)
        for r in range(n_sub):
            pooled = _pool_rows(u, halo, window, s * tm + r * sub)
            halo = u[sub - POOL_HALO:, :]
            u_next = project(r + 1, wu_ref) if r + 1 < n_sub else None
            y = jnp.dot(pooled.astype(BF16), wgrp_ref[0], preferred_element_type=F32)
            gate_next = project(r + 1, wg_ref) if r + 1 < n_sub else None
            bm_ref[r * sub:(r + 1) * sub, :] = ((y * scale_ref[...]) * _silu(gate)).astype(BF16)
            u, gate = u_next, gate_next
        halo_ref[j] = halo

    @pl.when(j >= n_mix_steps)
    def _():
        _attention_step(project, wcq_ref, wcg_ref, k_ref, v_ref, bc_ref, n_sub, sub)


def _pool_layer(x, g, w_in, w_grp, scale, kv, *, batch, seq, n_mem, tm, sub):
    M, D = x.shape
    assert all(w & (w - 1) == 0 for w in POOL_WINDOWS) and seq % tm == 0 and tm % sub == 0
    ca_heads_per_step = 2
    C = POOL_GROUP
    wc = ca_heads_per_step * CA_HEAD_DIM
    n_mix, n_ca = len(POOL_WINDOWS), E_CA // wc
    spt = seq // tm
    gate0 = E_MIX + E_CA
    w_u = w_in[:, :E_MIX].astype(BF16)
    w_qc = w_in[:, E_MIX:gate0].astype(BF16)
    w_gm = w_in[:, gate0:gate0 + E_MIX].astype(BF16)
    w_gc = w_in[:, gate0 + E_MIX:].astype(BF16)
    mix_blk = lambda j: jnp.minimum(j, n_mix - 1)
    ca_blk = lambda j: jnp.clip(j - n_mix, 0, n_ca - 1)
    mix_w = pl.BlockSpec((D, C), lambda b, s, j: (0, mix_blk(j)))
    ca_w = pl.BlockSpec((D, wc), lambda b, s, j: (0, ca_blk(j)))
    blocks = ([((tm, D), F32), ((1, D), F32)] + 2 * [((D, C), BF16)] + 2 * [((D, wc), BF16)]
              + [((1, C, C), BF16), ((1, C), F32), ((n_mem, wc), BF16), ((n_mem, wc), BF16),
                 ((tm, C), BF16), ((tm, wc), BF16)])
    scratch = [((tm, D), BF16), ((n_mix, POOL_HALO, C), F32)]
    return pl.pallas_call(
        functools.partial(_pool_layer_kernel, sub=sub, n_mix_steps=n_mix, row_chunk=128),
        grid=(batch, spt, n_mix + n_ca),
        in_specs=[pl.BlockSpec((tm, D), lambda b, s, j: (b * spt + s, 0)),
                  pl.BlockSpec((1, D), lambda b, s, j: (0, 0)),
                  mix_w, mix_w, ca_w, ca_w,
                  pl.BlockSpec((1, C, C), lambda b, s, j: (mix_blk(j), 0, 0)),
                  pl.BlockSpec((1, C), lambda b, s, j: (0, mix_blk(j))),
                  pl.BlockSpec((n_mem, wc), lambda b, s, j: (b, ca_blk(j))),
                  pl.BlockSpec((n_mem, wc), lambda b, s, j: (b, E_CA // wc + ca_blk(j)))],
        out_specs=[pl.BlockSpec((tm, C), lambda b, s, j: (b * spt + s, mix_blk(j))),
                   pl.BlockSpec((tm, wc), lambda b, s, j: (b * spt + s, ca_blk(j)))],
        out_shape=[jax.ShapeDtypeStruct((M, E_MIX), BF16), jax.ShapeDtypeStruct((M, E_CA), BF16)],
        scratch_shapes=[pltpu.VMEM(s_, d_) for s_, d_ in scratch],
        compiler_params=_compiler_params(("parallel", "arbitrary", "arbitrary"), blocks, scratch,
                                         temps=12 * _nbytes((sub + POOL_HALO, C), F32)),
        name="pool_layer",
    )(x, g.reshape(1, D), w_u, w_gm, w_qc, w_gc, w_grp.astype(BF16), scale.reshape(1, E_MIX),
      kv, kv)


def _cumsum_rows(x):
    rows, width = x.shape
    r = lax.broadcasted_iota(jnp.int32, (F32_SUBLANES, width), 0)
    k = 1
    while k < F32_SUBLANES:
        sh = pltpu.roll(x, k, axis=0)
        top = jnp.where(r >= k, sh[:F32_SUBLANES], 0.0)
        x = x + jnp.concatenate([top, sh[F32_SUBLANES:]], axis=0)
        k *= 2
    while k < rows:
        x = jnp.concatenate([x[:k], x[k:] + x[:rows - k]], axis=0)
        k *= 2
    return x


def _hgrn_rows(qx, fx, vx, lb, states, causal, between):
    rows = qx.shape[0]
    assert rows == 2 * HG_SPAN
    qs = _silu(qx) * (HG_HEAD_DIM ** -0.5)
    fg = lb + (1.0 - lb) * jax.nn.sigmoid(fx)
    kk = 1.0 - fg
    lf = jnp.log(fg)
    between(0)
    sp, ch = HG_SPAN, HG_CHUNK
    g = [_cumsum_rows(lf[:sp]), _cumsum_rows(lf[sp:])]
    cat = lambda parts: jnp.concatenate(parts, axis=0)
    e_in = jnp.exp(cat(g))
    e_out = jnp.exp(cat([gp[sp - 1:] - gp for gp in g]))
    d_q = jnp.exp(cat([gp - gp[ch - 1:ch] for gp in g]))
    d_k = jnp.exp(cat([gp[ch - 1:ch] - gp for gp in g]))
    dec0, dec1 = e_in[sp - 1:sp], e_in[rows - 1:rows]
    q_in, k_out = qs * e_in, kk * e_out
    q_dg, k_dg = (qs * d_q).astype(BF16), (kk * d_k).astype(BF16)
    q_off, k_off = q_in[sp:].astype(BF16), k_out[:sp].astype(BF16)
    q_st = cat([q_in[:sp], q_in[sp:] * dec0]).astype(BF16)
    k_st = cat([k_out[:sp] * dec1, k_out[sp:]]).astype(BF16)
    dec = dec0 * dec1
    between(1)
    heads = range(len(states))
    lanes = [slice(h * HG_HEAD_DIM, (h + 1) * HG_HEAD_DIM) for h in heads]
    attn = []
    for h in heads:
        sl = lanes[h]
        a_dg = lax.dot_general(q_dg[:, sl], k_dg[:, sl], NT_DIMS, preferred_element_type=F32)
        a_dg = jnp.where(causal, a_dg, 0.0)
        a_off = lax.dot_general(q_off[:, sl], k_off[:, sl], NT_DIMS, preferred_element_type=F32)
        attn.append(cat([a_dg[:sp], jnp.concatenate([a_off, a_dg[sp:, sp:]], axis=1)]).astype(BF16))
    between(2)
    outs, new_states = [], []
    for h in heads:
        sl, st = lanes[h], states[h]
        o_h = (jnp.dot(attn[h], vx[:, sl], preferred_element_type=F32)
               + lax.dot_general(q_st[:, sl], st.astype(BF16), NT_DIMS, preferred_element_type=F32))
        new_states.append(st * dec[:, sl] + lax.dot_general(vx[:, sl], k_st[:, sl], TN_DIMS,
                                                            preferred_element_type=F32))
        ms = jnp.mean(o_h * o_h, axis=-1, keepdims=True)
        outs.append(o_h * lax.rsqrt(ms + EPS))
    return jnp.concatenate(outs, axis=1), new_states


def _hgrn_layer_kernel(h_ref, wq_ref, wf_ref, wi_ref, wg_ref, wcq_ref, wcg_ref,
                       lb_ref, ng_ref, k_ref, v_ref, bm_ref, bc_ref, st_ref,
                       *, layer, sub, n_mix_steps):
    s = pl.program_id(1)
    j = pl.program_id(2)
    tm = h_ref.shape[0]
    n_sub = tm // sub
    heads = bm_ref.shape[1] // HG_HEAD_DIM

    def project(r, w_ref):
        return jnp.dot(h_ref[r * sub:(r + 1) * sub, :], w_ref[...], preferred_element_type=F32)

    @pl.when((j == 0) & (s == 0))
    def _():
        st_ref[...] = jnp.zeros_like(st_ref)

    @pl.when(j < n_mix_steps)
    def _():
        sm = jax.nn.softmax(lb_ref[...], axis=0)
        lb = jnp.sum(sm[:layer + 1], axis=0, keepdims=True) - sm[0:1]
        r_i = lax.broadcasted_iota(jnp.int32, (sub, sub), 0)
        c_i = lax.broadcasted_iota(jnp.int32, (sub, sub), 1)
        causal = (r_i // HG_SPAN == c_i // HG_SPAN) & (c_i <= r_i)
        states = [st_ref[j * heads + h] for h in range(heads)]
        w_refs = (wq_ref, wf_ref, wi_ref, wg_ref)
        cur = [project(0, w) for w in w_refs]
        for r in range(n_sub):
            nxt = [None] * len(w_refs)

            def between(k, r=r, nxt=nxt):
                if r + 1 < n_sub:
                    nxt[k] = project(r + 1, w_refs[k])

            qx, fx, ix, gx = cur
            o, states = _hgrn_rows(qx, fx, ix.astype(BF16), lb, states, causal, between)
            between(3)
            bm_ref[r * sub:(r + 1) * sub, :] = ((o * ng_ref[...]) * _silu(gx)).astype(BF16)
            cur = nxt
        for h in range(heads):
            st_ref[j * heads + h] = states[h]

    @pl.when(j >= n_mix_steps)
    def _():
        _attention_step(project, wcq_ref, wcg_ref, k_ref, v_ref, bc_ref, n_sub, sub)


def _hgrn_layer(h, w_in, lb_all, norm_g, kv, *, layer, batch, seq, n_mem, tm, sub):
    M, D = h.shape
    heads_per_step, ca_heads_per_step = 2, 2
    W = heads_per_step * HG_HEAD_DIM
    wc = ca_heads_per_step * CA_HEAD_DIM
    assert seq % tm == 0 and tm % sub == 0 and sub == 2 * HG_SPAN
    n_mix, n_ca = E_MIX // W, E_CA // wc
    spt = seq // tm
    mix_blk = lambda j: jnp.minimum(j, n_mix - 1)
    ca_blk = lambda j: jnp.clip(j - n_mix, 0, n_ca - 1)
    gate0 = 3 * E_MIX + E_CA
    assert gate0 % W == 0 and (3 * E_MIX) % wc == 0 and (gate0 + E_MIX) % wc == 0
    mix_w = lambda col0: pl.BlockSpec((D, W), lambda b, s, j: (0, col0 // W + mix_blk(j)))
    ca_w = lambda col0: pl.BlockSpec((D, wc), lambda b, s, j: (0, col0 // wc + ca_blk(j)))
    blocks = ([((tm, D), BF16)] + 4 * [((D, W), BF16)] + 2 * [((D, wc), BF16)]
              + [((DEPTH, W), F32), ((1, W), F32), ((n_mem, wc), BF16), ((n_mem, wc), BF16),
                 ((tm, W), BF16), ((tm, wc), BF16)])
    scratch = [((E_MIX // HG_HEAD_DIM, HG_HEAD_DIM, HG_HEAD_DIM), F32)]
    return pl.pallas_call(
        functools.partial(_hgrn_layer_kernel, layer=layer, sub=sub, n_mix_steps=n_mix),
        grid=(batch, spt, n_mix + n_ca),
        in_specs=[pl.BlockSpec((tm, D), lambda b, s, j: (b * spt + s, 0)),
                  mix_w(0), mix_w(E_MIX), mix_w(2 * E_MIX), mix_w(gate0),
                  ca_w(3 * E_MIX), ca_w(gate0 + E_MIX),
                  pl.BlockSpec((DEPTH, W), lambda b, s, j: (0, mix_blk(j))),
                  pl.BlockSpec((1, W), lambda b, s, j: (0, mix_blk(j))),
                  pl.BlockSpec((n_mem, wc), lambda b, s, j: (b, ca_blk(j))),
                  pl.BlockSpec((n_mem, wc), lambda b, s, j: (b, E_CA // wc + ca_blk(j)))],
        out_specs=[pl.BlockSpec((tm, W), lambda b, s, j: (b * spt + s, mix_blk(j))),
                   pl.BlockSpec((tm, wc), lambda b, s, j: (b * spt + s, ca_blk(j)))],
        out_shape=[jax.ShapeDtypeStruct((M, E_MIX), BF16), jax.ShapeDtypeStruct((M, E_CA), BF16)],
        scratch_shapes=[pltpu.VMEM(s_, d_) for s_, d_ in scratch],
        compiler_params=_compiler_params(("parallel", "arbitrary", "arbitrary"), blocks, scratch,
                                         temps=32 * _nbytes((sub, W), F32)),
        name="hgrn_layer",
    )(h, w_in, w_in, w_in, w_in, w_in, w_in, lb_all, norm_g.reshape(1, E_MIX), kv, kv)


def _out_proj_kernel(x_ref, bm_ref, bc_ref, wm_ref, wc_ref, g_ref, *out_refs, last):
    acc = jnp.dot(bm_ref[...], wm_ref[...], preferred_element_type=F32)
    acc = acc + jnp.dot(bc_ref[...], wc_ref[...], preferred_element_type=F32)
    y = x_ref[...] + acc
    normed = _rms_rows(y, g_ref[...])
    if last:
        out_refs[0][...] = normed
    else:
        out_refs[0][...] = y
        out_refs[1][...] = normed.astype(BF16)


def _out_proj(x, bm, bc, w_out, g, *, tm, last):
    M, D = x.shape
    assert M % tm == 0
    resident = pl.Buffered(1)
    row_blk = lambda w: pl.BlockSpec((tm, w), lambda i: (i, 0))
    out_specs, out_shape = [row_blk(D)], [jax.ShapeDtypeStruct((M, D), F32)]
    blocks = [((tm, D), F32), ((tm, E_MIX), BF16), ((tm, E_CA), BF16), ((1, D), F32), ((tm, D), F32)]
    if not last:
        out_specs.append(row_blk(D))
        out_shape.append(jax.ShapeDtypeStruct((M, D), BF16))
        blocks.append(((tm, D), BF16))
    single = [((E_MIX, D), BF16), ((E_CA, D), BF16)]
    return pl.pallas_call(
        functools.partial(_out_proj_kernel, last=last),
        grid=(M // tm,),
        in_specs=[row_blk(D), row_blk(E_MIX), row_blk(E_CA),
                  pl.BlockSpec((E_MIX, D), lambda i: (0, 0), pipeline_mode=resident),
                  pl.BlockSpec((E_CA, D), lambda i: (E_MIX // E_CA, 0), pipeline_mode=resident),
                  pl.BlockSpec((1, D), lambda i: (0, 0))],
        out_specs=out_specs,
        out_shape=out_shape,
        compiler_params=_compiler_params(("parallel",), blocks, single,
                                         temps=2 * _nbytes((tm, D), F32)),
        name="out_proj_last" if last else "out_proj",
    )(x, bm, bc, w_out, w_out, g.reshape(1, D))


def kernel(x, mem, norm_g, mem_norm_g, w_kv, w_out, pool_w_in, pool_w_grp, pool_scale,
           hgrn_w_in, hgrn_lb, hgrn_norm_g, final_g):
    batch, seq, d_model = x.shape
    n_mem = mem.shape[1]
    x2 = x.reshape(batch * seq, d_model)
    mem2 = mem.reshape(batch * n_mem, d_model)
    for i in range(DEPTH):
        j = i // N_MIXERS
        kv, = _norm_matmul(mem2, mem_norm_g, w_kv[i].astype(BF16), [(2 * E_CA, BF16)],
                           tm=512, tn=1024, name=f"kv_proj{i}")
        if i % N_MIXERS == 0:
            bm, bc = _pool_layer(x2, norm_g[i], pool_w_in[j], pool_w_grp[j], pool_scale[j], kv,
                                 batch=batch, seq=seq, n_mem=n_mem, tm=1024, sub=256)
        else:
            bm, bc = _hgrn_layer(h, hgrn_w_in[j].astype(BF16), hgrn_lb, hgrn_norm_g[j], kv,
                                 layer=i, batch=batch, seq=seq, n_mem=n_mem, tm=1024, sub=256)
        last = i == DEPTH - 1
        outs = _out_proj(x2, bm, bc, w_out[i].astype(BF16), final_g if last else norm_g[i + 1],
                         tm=512, last=last)
        if not last:
            x2, h = outs
    return outs[0].reshape(batch, seq, d_model)
```

```python
import functools

import jax
import jax.numpy as jnp
from jax import lax
from jax.experimental import pallas as pl
from jax.experimental.pallas import tpu as pltpu

F32 = jnp.float32
BF16 = jnp.bfloat16

DEPTH = 2
N_MIXERS = 2
E_MIX = 3072
E_CA = 1024
E_BRANCH = E_MIX + E_CA
CA_HEADS = 4
CA_HEAD_DIM = E_CA // CA_HEADS
POOL_WINDOWS = (2, 4, 8, 16)
POOL_GROUP = E_MIX // len(POOL_WINDOWS)
POOL_HALO = max(POOL_WINDOWS)
HG_HEAD_DIM = 128
HG_CHUNK = 64
HG_SPAN = 2 * HG_CHUNK
EPS = 1e-6

V7X_VMEM_BYTES = 64 * 1024 * 1024
F32_SUBLANES = 8
NT_DIMS = (((1,), (1,)), ((), ()))
TN_DIMS = (((0,), (0,)), ((), ()))


def _nbytes(shape, dtype):
    n = 1
    for s in shape:
        n *= s
    return n * jnp.dtype(dtype).itemsize


def _compiler_params(semantics, blocks, scratch=(), temps=0):
    need = 2 * sum(_nbytes(s, d) for s, d in blocks) + sum(_nbytes(s, d) for s, d in scratch) + temps
    limit = min(need + need // 4 + (2 << 20), V7X_VMEM_BYTES - (4 << 20))
    return pltpu.CompilerParams(dimension_semantics=semantics, vmem_limit_bytes=int(limit))


def _silu(x):
    return x * jax.nn.sigmoid(x)


def _rms_rows(x, g):
    ms = jnp.mean(x * x, axis=-1, keepdims=True)
    return (x * lax.rsqrt(ms + EPS)) * g


def _norm_rows_to(h_ref, x_ref, g_ref, row_chunk):
    def body(r, carry):
        rows = pl.ds(pl.multiple_of(r * row_chunk, row_chunk), row_chunk)
        h_ref[rows, :] = _rms_rows(x_ref[rows, :], g_ref[...]).astype(BF16)
        return carry
    lax.fori_loop(0, x_ref.shape[0] // row_chunk, body, 0)


def _attend(q, k, v):
    s = lax.dot_general(q, k, NT_DIMS, preferred_element_type=F32) * (CA_HEAD_DIM ** -0.5)
    e = jnp.exp(s - jnp.max(s, axis=-1, keepdims=True))
    p = e / jnp.sum(e, axis=-1, keepdims=True)
    return jnp.dot(p.astype(BF16), v, preferred_element_type=F32)


def _kv_proj_kernel(x_ref, g_ref, w_ref, o_ref, h_ref, *, row_chunk):
    @pl.when((pl.program_id(1) == 0) & (pl.program_id(2) == 0))
    def _():
        _norm_rows_to(h_ref, x_ref, g_ref, row_chunk)

    acc = jnp.dot(h_ref[...], w_ref[...].astype(BF16), preferred_element_type=F32)
    o_ref[...] = acc.astype(o_ref.dtype)


def _kv_proj(mem, g, w_kv, *, tm, tn):
    M, D = mem.shape
    L, _, N = w_kv.shape
    assert M % tm == 0 and N % tn == 0
    blocks = [((tm, D), F32), ((1, D), F32), ((D, tn), w_kv.dtype), ((tm, tn), BF16)]
    return pl.pallas_call(
        functools.partial(_kv_proj_kernel, row_chunk=128),
        grid=(M // tm, L, N // tn),
        in_specs=[pl.BlockSpec((tm, D), lambda i, l, j: (i, 0)),
                  pl.BlockSpec((1, D), lambda i, l, j: (0, 0)),
                  pl.BlockSpec((None, D, tn), lambda i, l, j: (l, 0, j))],
        out_specs=pl.BlockSpec((tm, tn), lambda i, l, j: (i, l * (N // tn) + j)),
        out_shape=jax.ShapeDtypeStruct((M, L * N), BF16),
        scratch_shapes=[pltpu.VMEM((tm, D), BF16)],
        compiler_params=_compiler_params(("parallel", "arbitrary", "arbitrary"), blocks,
                                         [((tm, D), BF16)], temps=2 * _nbytes((tm, tn), F32)),
        name="kv_proj",
    )(mem, g.reshape(1, D), w_kv)


def _attention_step(project, wcq_ref, wcg_ref, k_ref, v_ref, bc_ref, n_sub, sub):
    for r in range(n_sub):
        q = project(r, wcq_ref).astype(BF16)
        gate = project(r, wcg_ref)
        for hh in range(bc_ref.shape[1] // CA_HEAD_DIM):
            sl = slice(hh * CA_HEAD_DIM, (hh + 1) * CA_HEAD_DIM)
            o = _attend(q[:, sl], k_ref[:, sl], v_ref[:, sl])
            bc_ref[r * sub:(r + 1) * sub, sl] = (o * _silu(gate[:, sl])).astype(BF16)


def _pool_rows(u, halo, window, pos0):
    s = jnp.concatenate([halo, u], axis=0)
    k = 1
    while k < POOL_HALO:
        s = jnp.where(k < window, s + pltpu.roll(s, k, axis=0), s)
        k *= 2
    pos = pos0 + lax.broadcasted_iota(jnp.int32, u.shape, 0)
    cnt = jnp.minimum(pos + 1, window).astype(F32)
    return s[POOL_HALO:, :] / cnt - u


def _pool_layer_kernel(x_ref, g_ref, wu_ref, wg_ref, wcq_ref, wcg_ref, wgrp_ref, scale_ref,
                       k_ref, v_ref, bm_ref, bc_ref, h_ref, halo_ref,
                       *, sub, n_mix_steps, row_chunk):
    s = pl.program_id(1)
    j = pl.program_id(2)
    tm = x_ref.shape[0]
    n_sub = tm // sub

    def project(r, w_ref):
        return jnp.dot(h_ref[r * sub:(r + 1) * sub, :], w_ref[...], preferred_element_type=F32)

    @pl.when(j == 0)
    def _():
        _norm_rows_to(h_ref, x_ref, g_ref, row_chunk)

    @pl.when(j < n_mix_steps)
    def _():
        window = jnp.int32(POOL_WINDOWS[0])
        for g in range(1, len(POOL_WINDOWS)):
            window = jnp.where(j == g, POOL_WINDOWS[g], window)
        halo = jnp.where(s == 0, 0.0, halo_ref[j])
        u, gate = project(0, wu_ref), project(0, wg_ref)
        for r in range(n_sub):
            pooled = _pool_rows(u, halo, window, s * tm + r * sub)
            halo = u[sub - POOL_HALO:, :]
            u_next = project(r + 1, wu_ref) if r + 1 < n_sub else None
            y = jnp.dot(pooled.astype(BF16), wgrp_ref[0], preferred_element_type=F32)
            gate_next = project(r + 1, wg_ref) if r + 1 < n_sub else None
            bm_ref[r * sub:(r + 1) * sub, :] = ((y * scale_ref[...]) * _silu(gate)).astype(BF16)
            u, gate = u_next, gate_next
        halo_ref[j] = halo

    @pl.when(j >= n_mix_steps)
    def _():
        _attention_step(project, wcq_ref, wcg_ref, k_ref, v_ref, bc_ref, n_sub, sub)


def _pool_layer(x, g, w_in, w_grp, scale, kv, *, kv_col0, batch, seq, n_mem, tm, sub):
    M, D = x.shape
    assert all(w & (w - 1) == 0 for w in POOL_WINDOWS) and seq % tm == 0 and tm % sub == 0
    ca_heads_per_step = 2
    C = POOL_GROUP
    wc = ca_heads_per_step * CA_HEAD_DIM
    n_mix, n_ca = len(POOL_WINDOWS), E_CA // wc
    spt = seq // tm
    gate0 = E_MIX + E_CA
    w_u = w_in[:, :E_MIX].astype(BF16)
    w_qc = w_in[:, E_MIX:gate0].astype(BF16)
    w_gm = w_in[:, gate0:gate0 + E_MIX].astype(BF16)
    w_gc = w_in[:, gate0 + E_MIX:].astype(BF16)
    mix_blk = lambda j: jnp.minimum(j, n_mix - 1)
    ca_blk = lambda j: jnp.clip(j - n_mix, 0, n_ca - 1)
    mix_w = pl.BlockSpec((D, C), lambda b, s, j: (0, mix_blk(j)))
    ca_w = pl.BlockSpec((D, wc), lambda b, s, j: (0, ca_blk(j)))
    blocks = ([((tm, D), F32), ((1, D), F32)] + 2 * [((D, C), BF16)] + 2 * [((D, wc), BF16)]
              + [((1, C, C), BF16), ((1, C), F32), ((n_mem, wc), BF16), ((n_mem, wc), BF16),
                 ((tm, C), BF16), ((tm, wc), BF16)])
    scratch = [((tm, D), BF16), ((n_mix, POOL_HALO, C), F32)]
    return pl.pallas_call(
        functools.partial(_pool_layer_kernel, sub=sub, n_mix_steps=n_mix, row_chunk=128),
        grid=(batch, spt, n_mix + n_ca),
        in_specs=[pl.BlockSpec((tm, D), lambda b, s, j: (b * spt + s, 0)),
                  pl.BlockSpec((1, D), lambda b, s, j: (0, 0)),
                  mix_w, mix_w, ca_w, ca_w,
                  pl.BlockSpec((1, C, C), lambda b, s, j: (mix_blk(j), 0, 0)),
                  pl.BlockSpec((1, C), lambda b, s, j: (0, mix_blk(j))),
                  pl.BlockSpec((n_mem, wc), lambda b, s, j: (b, kv_col0 // wc + ca_blk(j))),
                  pl.BlockSpec((n_mem, wc),
                               lambda b, s, j: (b, (kv_col0 + E_CA) // wc + ca_blk(j)))],
        out_specs=[pl.BlockSpec((tm, C), lambda b, s, j: (b * spt + s, mix_blk(j))),
                   pl.BlockSpec((tm, wc), lambda b, s, j: (b * spt + s, ca_blk(j)))],
        out_shape=[jax.ShapeDtypeStruct((M, E_MIX), BF16), jax.ShapeDtypeStruct((M, E_CA), BF16)],
        scratch_shapes=[pltpu.VMEM(s_, d_) for s_, d_ in scratch],
        compiler_params=_compiler_params(("parallel", "arbitrary", "arbitrary"), blocks, scratch,
                                         temps=12 * _nbytes((sub + POOL_HALO, C), F32)),
        name="pool_layer",
    )(x, g.reshape(1, D), w_u, w_gm, w_qc, w_gc, w_grp.astype(BF16), scale.reshape(1, E_MIX),
      kv, kv)


def _cumsum_rows(x):
    rows, width = x.shape
    r = lax.broadcasted_iota(jnp.int32, (F32_SUBLANES, width), 0)
    k = 1
    while k < F32_SUBLANES:
        sh = pltpu.roll(x, k, axis=0)
        top = jnp.where(r >= k, sh[:F32_SUBLANES], 0.0)
        x = x + jnp.concatenate([top, sh[F32_SUBLANES:]], axis=0)
        k *= 2
    while k < rows:
        x = jnp.concatenate([x[:k], x[k:] + x[:rows - k]], axis=0)
        k *= 2
    return x


def _hgrn_rows(qx, fx, vx, lb, states, causal, between):
    rows = qx.shape[0]
    assert rows == 2 * HG_SPAN
    qs = _silu(qx) * (HG_HEAD_DIM ** -0.5)
    fg = lb + (1.0 - lb) * jax.nn.sigmoid(fx)
    kk = 1.0 - fg
    lf = jnp.log(fg)
    between(0)
    sp, ch = HG_SPAN, HG_CHUNK
    g = [_cumsum_rows(lf[:sp]), _cumsum_rows(lf[sp:])]
    cat = lambda parts: jnp.concatenate(parts, axis=0)
    e_in = jnp.exp(cat(g))
    e_out = jnp.exp(cat([gp[sp - 1:] - gp for gp in g]))
    d_q = jnp.exp(cat([gp - gp[ch - 1:ch] for gp in g]))
    d_k = jnp.exp(cat([gp[ch - 1:ch] - gp for gp in g]))
    dec0, dec1 = e_in[sp - 1:sp], e_in[rows - 1:rows]
    q_in, k_out = qs * e_in, kk * e_out
    q_dg, k_dg = (qs * d_q).astype(BF16), (kk * d_k).astype(BF16)
    q_off, k_off = q_in[sp:].astype(BF16), k_out[:sp].astype(BF16)
    q_st = cat([q_in[:sp], q_in[sp:] * dec0]).astype(BF16)
    k_st = cat([k_out[:sp] * dec1, k_out[sp:]]).astype(BF16)
    dec = dec0 * dec1
    between(1)
    heads = range(len(states))
    lanes = [slice(h * HG_HEAD_DIM, (h + 1) * HG_HEAD_DIM) for h in heads]
    attn = []
    for h in heads:
        sl = lanes[h]
        a_dg = lax.dot_general(q_dg[:, sl], k_dg[:, sl], NT_DIMS, preferred_element_type=F32)
        a_dg = jnp.where(causal, a_dg, 0.0)
        a_off = lax.dot_general(q_off[:, sl], k_off[:, sl], NT_DIMS, preferred_element_type=F32)
        attn.append(cat([a_dg[:sp], jnp.concatenate([a_off, a_dg[sp:, sp:]], axis=1)]).astype(BF16))
    between(2)
    outs, new_states = [], []
    for h in heads:
        sl, st = lanes[h], states[h]
        o_h = (jnp.dot(attn[h], vx[:, sl], preferred_element_type=F32)
               + lax.dot_general(q_st[:, sl], st.astype(BF16), NT_DIMS, preferred_element_type=F32))
        new_states.append(st * dec[:, sl] + lax.dot_general(vx[:, sl], k_st[:, sl], TN_DIMS,
                                                            preferred_element_type=F32))
        ms = jnp.mean(o_h * o_h, axis=-1, keepdims=True)
        outs.append(o_h * lax.rsqrt(ms + EPS))
    return jnp.concatenate(outs, axis=1), new_states


def _hgrn_layer_kernel(h_ref, wq_ref, wf_ref, wi_ref, wg_ref, wcq_ref, wcg_ref,
                       lb_ref, ng_ref, k_ref, v_ref, bm_ref, bc_ref, st_ref,
                       *, layer, sub, n_mix_steps):
    s = pl.program_id(1)
    j = pl.program_id(2)
    tm = h_ref.shape[0]
    n_sub = tm // sub
    heads = bm_ref.shape[1] // HG_HEAD_DIM

    def project(r, w_ref):
        return jnp.dot(h_ref[r * sub:(r + 1) * sub, :], w_ref[...], preferred_element_type=F32)

    @pl.when((j == 0) & (s == 0))
    def _():
        st_ref[...] = jnp.zeros_like(st_ref)

    @pl.when(j < n_mix_steps)
    def _():
        sm = jax.nn.softmax(lb_ref[...], axis=0)
        lb = jnp.sum(sm[:layer + 1], axis=0, keepdims=True) - sm[0:1]
        r_i = lax.broadcasted_iota(jnp.int32, (sub, sub), 0)
        c_i = lax.broadcasted_iota(jnp.int32, (sub, sub), 1)
        causal = (r_i // HG_SPAN == c_i // HG_SPAN) & (c_i <= r_i)
        states = [st_ref[j * heads + h] for h in range(heads)]
        w_refs = (wq_ref, wf_ref, wi_ref, wg_ref)
        cur = [project(0, w) for w in w_refs]
        for r in range(n_sub):
            nxt = [None] * len(w_refs)

            def between(k, r=r, nxt=nxt):
                if r + 1 < n_sub:
                    nxt[k] = project(r + 1, w_refs[k])

            qx, fx, ix, gx = cur
            o, states = _hgrn_rows(qx, fx, ix.astype(BF16), lb, states, causal, between)
            between(3)
            bm_ref[r * sub:(r + 1) * sub, :] = ((o * ng_ref[...]) * _silu(gx)).astype(BF16)
            cur = nxt
        for h in range(heads):
            st_ref[j * heads + h] = states[h]

    @pl.when(j >= n_mix_steps)
    def _():
        _attention_step(project, wcq_ref, wcg_ref, k_ref, v_ref, bc_ref, n_sub, sub)


def _hgrn_layer(h, w_in, lb_all, norm_g, kv, *, kv_col0, layer, batch, seq, n_mem, tm, sub):
    M, D = h.shape
    heads_per_step, ca_heads_per_step = 2, 2
    W = heads_per_step * HG_HEAD_DIM
    wc = ca_heads_per_step * CA_HEAD_DIM
    assert seq % tm == 0 and tm % sub == 0 and sub == 2 * HG_SPAN
    n_mix, n_ca = E_MIX // W, E_CA // wc
    spt = seq // tm
    mix_blk = lambda j: jnp.minimum(j, n_mix - 1)
    ca_blk = lambda j: jnp.clip(j - n_mix, 0, n_ca - 1)
    gate0 = 3 * E_MIX + E_CA
    assert gate0 % W == 0 and (3 * E_MIX) % wc == 0 and (gate0 + E_MIX) % wc == 0
    mix_w = lambda col0: pl.BlockSpec((D, W), lambda b, s, j: (0, col0 // W + mix_blk(j)))
    ca_w = lambda col0: pl.BlockSpec((D, wc), lambda b, s, j: (0, col0 // wc + ca_blk(j)))
    blocks = ([((tm, D), BF16)] + 4 * [((D, W), BF16)] + 2 * [((D, wc), BF16)]
              + [((DEPTH, W), F32), ((1, W), F32), ((n_mem, wc), BF16), ((n_mem, wc), BF16),
                 ((tm, W), BF16), ((tm, wc), BF16)])
    scratch = [((E_MIX // HG_HEAD_DIM, HG_HEAD_DIM, HG_HEAD_DIM), F32)]
    return pl.pallas_call(
        functools.partial(_hgrn_layer_kernel, layer=layer, sub=sub, n_mix_steps=n_mix),
        grid=(batch, spt, n_mix + n_ca),
        in_specs=[pl.BlockSpec((tm, D), lambda b, s, j: (b * spt + s, 0)),
                  mix_w(0), mix_w(E_MIX), mix_w(2 * E_MIX), mix_w(gate0),
                  ca_w(3 * E_MIX), ca_w(gate0 + E_MIX),
                  pl.BlockSpec((DEPTH, W), lambda b, s, j: (0, mix_blk(j))),
                  pl.BlockSpec((1, W), lambda b, s, j: (0, mix_blk(j))),
                  pl.BlockSpec((n_mem, wc), lambda b, s, j: (b, kv_col0 // wc + ca_blk(j))),
                  pl.BlockSpec((n_mem, wc),
                               lambda b, s, j: (b, (kv_col0 + E_CA) // wc + ca_blk(j)))],
        out_specs=[pl.BlockSpec((tm, W), lambda b, s, j: (b * spt + s, mix_blk(j))),
                   pl.BlockSpec((tm, wc), lambda b, s, j: (b * spt + s, ca_blk(j)))],
        out_shape=[jax.ShapeDtypeStruct((M, E_MIX), BF16), jax.ShapeDtypeStruct((M, E_CA), BF16)],
        scratch_shapes=[pltpu.VMEM(s_, d_) for s_, d_ in scratch],
        compiler_params=_compiler_params(("parallel", "arbitrary", "arbitrary"), blocks, scratch,
                                         temps=32 * _nbytes((sub, W), F32)),
        name="hgrn_layer",
    )(h, w_in, w_in, w_in, w_in, w_in, w_in, lb_all, norm_g.reshape(1, E_MIX), kv, kv)


def _out_proj_kernel(x_ref, bm_ref, bc_ref, wm_ref, wc_ref, g_ref, *out_refs, last):
    acc = jnp.dot(bm_ref[...], wm_ref[...], preferred_element_type=F32)
    acc = acc + jnp.dot(bc_ref[...], wc_ref[...], preferred_element_type=F32)
    y = x_ref[...] + acc
    normed = _rms_rows(y, g_ref[...])
    if last:
        out_refs[0][...] = normed
    else:
        out_refs[0][...] = y
        out_refs[1][...] = normed.astype(BF16)


def _out_proj(x, bm, bc, w_out, g, *, tm, last):
    M, D = x.shape
    assert M % tm == 0
    resident = pl.Buffered(1)
    row_blk = lambda w: pl.BlockSpec((tm, w), lambda i: (i, 0))
    out_specs, out_shape = [row_blk(D)], [jax.ShapeDtypeStruct((M, D), F32)]
    blocks = [((tm, D), F32), ((tm, E_MIX), BF16), ((tm, E_CA), BF16), ((1, D), F32), ((tm, D), F32)]
    if not last:
        out_specs.append(row_blk(D))
        out_shape.append(jax.ShapeDtypeStruct((M, D), BF16))
        blocks.append(((tm, D), BF16))
    single = [((E_MIX, D), BF16), ((E_CA, D), BF16)]
    return pl.pallas_call(
        functools.partial(_out_proj_kernel, last=last),
        grid=(M // tm,),
        in_specs=[row_blk(D), row_blk(E_MIX), row_blk(E_CA),
                  pl.BlockSpec((E_MIX, D), lambda i: (0, 0), pipeline_mode=resident),
                  pl.BlockSpec((E_CA, D), lambda i: (E_MIX // E_CA, 0), pipeline_mode=resident),
                  pl.BlockSpec((1, D), lambda i: (0, 0))],
        out_specs=out_specs,
        out_shape=out_shape,
        compiler_params=_compiler_params(("parallel",), blocks, single,
                                         temps=2 * _nbytes((tm, D), F32)),
        name="out_proj_last" if last else "out_proj",
    )(x, bm, bc, w_out, w_out, g.reshape(1, D))


def kernel(x, mem, norm_g, mem_norm_g, w_kv, w_out, pool_w_in, pool_w_grp, pool_scale,
           hgrn_w_in, hgrn_lb, hgrn_norm_g, final_g):
    batch, seq, d_model = x.shape
    n_mem = mem.shape[1]
    x2 = x.reshape(batch * seq, d_model)
    mem2 = mem.reshape(batch * n_mem, d_model)
    kv = _kv_proj(mem2, mem_norm_g, w_kv, tm=512, tn=1024)
    for i in range(DEPTH):
        j = i // N_MIXERS
        common = dict(kv_col0=i * 2 * E_CA, batch=batch, seq=seq, n_mem=n_mem, sub=256)
        if i % N_MIXERS == 0:
            bm, bc = _pool_layer(x2, norm_g[i], pool_w_in[j], pool_w_grp[j], pool_scale[j], kv,
                                 tm=1024, **common)
        else:
            bm, bc = _hgrn_layer(h, hgrn_w_in[j].astype(BF16), hgrn_lb, hgrn_norm_g[j], kv,
                                 layer=i, tm=2048, **common)
        last = i == DEPTH - 1
        outs = _out_proj(x2, bm, bc, w_out[i].astype(BF16), final_g if last else norm_g[i + 1],
                         tm=512, last=last)
        if not last:
            x2, h = outs
    return outs[0].reshape(batch, seq, d_model)
```

```python
import functools

import jax
import jax.numpy as jnp
from jax import lax
from jax.experimental import pallas as pl
from jax.experimental.pallas import tpu as pltpu

F32 = jnp.float32
BF16 = jnp.bfloat16

DEPTH = 2
N_MIXERS = 2
E_MIX = 3072
E_CA = 1024
E_BRANCH = E_MIX + E_CA
CA_HEADS = 4
CA_HEAD_DIM = E_CA // CA_HEADS
POOL_WINDOWS = (2, 4, 8, 16)
POOL_GROUP = E_MIX // len(POOL_WINDOWS)
POOL_HALO = max(POOL_WINDOWS)
HG_HEAD_DIM = 128
HG_CHUNK = 64
HG_SPAN = 2 * HG_CHUNK
EPS = 1e-6

V7X_VMEM_BYTES = 64 * 1024 * 1024
F32_SUBLANES = 8
NT_DIMS = (((1,), (1,)), ((), ()))
TN_DIMS = (((0,), (0,)), ((), ()))


def _nbytes(shape, dtype):
    n = 1
    for s in shape:
        n *= s
    return n * jnp.dtype(dtype).itemsize


def _compiler_params(semantics, blocks, scratch=(), temps=0):
    need = 2 * sum(_nbytes(s, d) for s, d in blocks) + sum(_nbytes(s, d) for s, d in scratch) + temps
    limit = min(need + need // 4 + (2 << 20), V7X_VMEM_BYTES - (4 << 20))
    return pltpu.CompilerParams(dimension_semantics=semantics, vmem_limit_bytes=int(limit))


def _silu(x):
    return x * jax.nn.sigmoid(x)


def _rms_rows(x, g):
    ms = jnp.mean(x * x, axis=-1, keepdims=True)
    return (x * lax.rsqrt(ms + EPS)) * g


def _norm_rows_to(h_ref, x_ref, g_ref, row_chunk):
    def body(r, carry):
        rows = pl.ds(pl.multiple_of(r * row_chunk, row_chunk), row_chunk)
        h_ref[rows, :] = _rms_rows(x_ref[rows, :], g_ref[...]).astype(BF16)
        return carry
    lax.fori_loop(0, x_ref.shape[0] // row_chunk, body, 0)


def _attend(q, k, v):
    s = lax.dot_general(q, k, NT_DIMS, preferred_element_type=F32) * (CA_HEAD_DIM ** -0.5)
    e = jnp.exp(s - jnp.max(s, axis=-1, keepdims=True))
    p = e / jnp.sum(e, axis=-1, keepdims=True)
    return jnp.dot(p.astype(BF16), v, preferred_element_type=F32)


def _side_cast_plan(srcs, step_of, n_steps):
    in_specs, out_specs, out_shapes, blocks, n_blocks = [], [], [], [], []
    for arr, lead, rows in srcs:
        R, N = arr.shape[-2:]
        nb = R // rows
        assert R % rows == 0 and nb <= n_steps
        blk = lambda *ids, nb=nb: jnp.minimum(step_of(*ids), nb - 1)
        if lead is None:
            in_specs.append(pl.BlockSpec((rows, N), lambda *ids, blk=blk: (blk(*ids), 0)))
        else:
            in_specs.append(pl.BlockSpec((None, rows, N),
                                         lambda *ids, blk=blk, lead=lead: (lead, blk(*ids), 0)))
        out_specs.append(pl.BlockSpec((rows, N), lambda *ids, blk=blk: (blk(*ids), 0)))
        out_shapes.append(jax.ShapeDtypeStruct((R, N), BF16))
        blocks += [((rows, N), arr.dtype), ((rows, N), BF16)]
        n_blocks.append(nb)
    return in_specs, out_specs, out_shapes, blocks, tuple(n_blocks)


def _side_cast(step, src_refs, dst_refs, n_blocks):
    for src, dst, nb in zip(src_refs, dst_refs, n_blocks):
        @pl.when(step < nb)
        def _(src=src, dst=dst):
            dst[...] = src[...].astype(BF16)


def _linear_step():
    step = pl.program_id(0)
    for axis in range(1, 3):
        step = step * pl.num_programs(axis) + pl.program_id(axis)
    return step


def _kv_proj_kernel(x_ref, g_ref, w_ref, o_ref, h_ref, *, row_chunk):
    @pl.when((pl.program_id(1) == 0) & (pl.program_id(2) == 0))
    def _():
        _norm_rows_to(h_ref, x_ref, g_ref, row_chunk)

    acc = jnp.dot(h_ref[...], w_ref[...].astype(BF16), preferred_element_type=F32)
    o_ref[...] = acc.astype(o_ref.dtype)


def _kv_proj(mem, g, w_kv, *, tm, tn):
    M, D = mem.shape
    L, _, N = w_kv.shape
    assert M % tm == 0 and N % tn == 0
    blocks = [((tm, D), F32), ((1, D), F32), ((D, tn), w_kv.dtype), ((tm, tn), BF16)]
    return pl.pallas_call(
        functools.partial(_kv_proj_kernel, row_chunk=128),
        grid=(M // tm, L, N // tn),
        in_specs=[pl.BlockSpec((tm, D), lambda i, l, j: (i, 0)),
                  pl.BlockSpec((1, D), lambda i, l, j: (0, 0)),
                  pl.BlockSpec((None, D, tn), lambda i, l, j: (l, 0, j))],
        out_specs=pl.BlockSpec((tm, tn), lambda i, l, j: (i, l * (N // tn) + j)),
        out_shape=jax.ShapeDtypeStruct((M, L * N), BF16),
        scratch_shapes=[pltpu.VMEM((tm, D), BF16)],
        compiler_params=_compiler_params(("parallel", "arbitrary", "arbitrary"), blocks,
                                         [((tm, D), BF16)], temps=2 * _nbytes((tm, tn), F32)),
        name="kv_proj",
    )(mem, g.reshape(1, D), w_kv)


def _attention_step(project, wcq_ref, wcg_ref, k_ref, v_ref, bc_ref, n_sub, sub):
    for r in range(n_sub):
        q = project(r, wcq_ref).astype(BF16)
        gate = project(r, wcg_ref)
        for hh in range(bc_ref.shape[1] // CA_HEAD_DIM):
            sl = slice(hh * CA_HEAD_DIM, (hh + 1) * CA_HEAD_DIM)
            o = _attend(q[:, sl], k_ref[:, sl], v_ref[:, sl])
            bc_ref[r * sub:(r + 1) * sub, sl] = (o * _silu(gate[:, sl])).astype(BF16)


def _pool_rows(u, halo, window, pos0):
    s = jnp.concatenate([halo, u], axis=0)
    k = 1
    while k < POOL_HALO:
        s = jnp.where(k < window, s + pltpu.roll(s, k, axis=0), s)
        k *= 2
    pos = pos0 + lax.broadcasted_iota(jnp.int32, u.shape, 0)
    cnt = jnp.minimum(pos + 1, window).astype(F32)
    return s[POOL_HALO:, :] / cnt - u


def _pool_layer_kernel(x_ref, g_ref, wu_ref, wg_ref, wcq_ref, wcg_ref, wgrp_ref, scale_ref,
                       k_ref, v_ref, *rest, sub, n_mix_steps, row_chunk, side_blocks):
    n_side = len(side_blocks)
    side_src, (bm_ref, bc_ref), rest = rest[:n_side], rest[n_side:n_side + 2], rest[n_side + 2:]
    side_dst, (h_ref, halo_ref) = rest[:n_side], rest[n_side:]
    s = pl.program_id(1)
    j = pl.program_id(2)
    tm = x_ref.shape[0]
    n_sub = tm // sub
    _side_cast(_linear_step(), side_src, side_dst, side_blocks)

    def project(r, w_ref):
        return jnp.dot(h_ref[r * sub:(r + 1) * sub, :], w_ref[...], preferred_element_type=F32)

    @pl.when(j == 0)
    def _():
        _norm_rows_to(h_ref, x_ref, g_ref, row_chunk)

    @pl.when(j < n_mix_steps)
    def _():
        window = jnp.int32(POOL_WINDOWS[0])
        for g in range(1, len(POOL_WINDOWS)):
            window = jnp.where(j == g, POOL_WINDOWS[g], window)
        halo = jnp.where(s == 0, 0.0, halo_ref[j])
        u, gate = project(0, wu_ref), project(0, wg_ref)
        for r in range(n_sub):
            pooled = _pool_rows(u, halo, window, s * tm + r * sub)
            halo = u[sub - POOL_HALO:, :]
            u_next = project(r + 1, wu_ref) if r + 1 < n_sub else None
            y = jnp.dot(pooled.astype(BF16), wgrp_ref[0], preferred_element_type=F32)
            gate_next = project(r + 1, wg_ref) if r + 1 < n_sub else None
            bm_ref[r * sub:(r + 1) * sub, :] = ((y * scale_ref[...]) * _silu(gate)).astype(BF16)
            u, gate = u_next, gate_next
        halo_ref[j] = halo

    @pl.when(j >= n_mix_steps)
    def _():
        _attention_step(project, wcq_ref, wcg_ref, k_ref, v_ref, bc_ref, n_sub, sub)


def _pool_layer(x, g, w_in, w_grp, scale, kv, side, *, kv_col0, batch, seq, n_mem, tm, sub):
    M, D = x.shape
    assert all(w & (w - 1) == 0 for w in POOL_WINDOWS) and seq % tm == 0 and tm % sub == 0
    ca_heads_per_step = 2
    C = POOL_GROUP
    wc = ca_heads_per_step * CA_HEAD_DIM
    n_mix, n_ca = len(POOL_WINDOWS), E_CA // wc
    spt = seq // tm
    gate0 = E_MIX + E_CA
    w_u = w_in[:, :E_MIX].astype(BF16)
    w_qc = w_in[:, E_MIX:gate0].astype(BF16)
    w_gm = w_in[:, gate0:gate0 + E_MIX].astype(BF16)
    w_gc = w_in[:, gate0 + E_MIX:].astype(BF16)
    mix_blk = lambda j: jnp.minimum(j, n_mix - 1)
    ca_blk = lambda j: jnp.clip(j - n_mix, 0, n_ca - 1)
    mix_w = pl.BlockSpec((D, C), lambda b, s, j: (0, mix_blk(j)))
    ca_w = pl.BlockSpec((D, wc), lambda b, s, j: (0, ca_blk(j)))
    blocks = ([((tm, D), F32), ((1, D), F32)] + 2 * [((D, C), BF16)] + 2 * [((D, wc), BF16)]
              + [((1, C, C), BF16), ((1, C), F32), ((n_mem, wc), BF16), ((n_mem, wc), BF16),
                 ((tm, C), BF16), ((tm, wc), BF16)])
    scratch = [((tm, D), BF16), ((n_mix, POOL_HALO, C), F32)]
    n_steps = n_mix + n_ca
    side_in, side_out, side_shapes, side_blk, side_blocks = _side_cast_plan(
        side, lambda b, s, j: (b * spt + s) * n_steps + j, batch * spt * n_steps)
    return pl.pallas_call(
        functools.partial(_pool_layer_kernel, sub=sub, n_mix_steps=n_mix, row_chunk=128,
                          side_blocks=side_blocks),
        grid=(batch, spt, n_steps),
        in_specs=[pl.BlockSpec((tm, D), lambda b, s, j: (b * spt + s, 0)),
                  pl.BlockSpec((1, D), lambda b, s, j: (0, 0)),
                  mix_w, mix_w, ca_w, ca_w,
                  pl.BlockSpec((1, C, C), lambda b, s, j: (mix_blk(j), 0, 0)),
                  pl.BlockSpec((1, C), lambda b, s, j: (0, mix_blk(j))),
                  pl.BlockSpec((n_mem, wc), lambda b, s, j: (b, kv_col0 // wc + ca_blk(j))),
                  pl.BlockSpec((n_mem, wc),
                               lambda b, s, j: (b, (kv_col0 + E_CA) // wc + ca_blk(j)))] + side_in,
        out_specs=[pl.BlockSpec((tm, C), lambda b, s, j: (b * spt + s, mix_blk(j))),
                   pl.BlockSpec((tm, wc), lambda b, s, j: (b * spt + s, ca_blk(j)))] + side_out,
        out_shape=[jax.ShapeDtypeStruct((M, E_MIX), BF16),
                   jax.ShapeDtypeStruct((M, E_CA), BF16)] + side_shapes,
        scratch_shapes=[pltpu.VMEM(s_, d_) for s_, d_ in scratch],
        compiler_params=_compiler_params(("arbitrary", "arbitrary", "arbitrary"),
                                         blocks + side_blk, scratch,
                                         temps=12 * _nbytes((sub + POOL_HALO, C), F32)),
        name="pool_layer",
    )(x, g.reshape(1, D), w_u, w_gm, w_qc, w_gc, w_grp.astype(BF16), scale.reshape(1, E_MIX),
      kv, kv, *[a for a, _, _ in side])


def _cumsum_rows(x):
    rows, width = x.shape
    r = lax.broadcasted_iota(jnp.int32, (F32_SUBLANES, width), 0)
    k = 1
    while k < F32_SUBLANES:
        sh = pltpu.roll(x, k, axis=0)
        top = jnp.where(r >= k, sh[:F32_SUBLANES], 0.0)
        x = x + jnp.concatenate([top, sh[F32_SUBLANES:]], axis=0)
        k *= 2
    while k < rows:
        x = jnp.concatenate([x[:k], x[k:] + x[:rows - k]], axis=0)
        k *= 2
    return x


def _hgrn_rows(qx, fx, vx, lb, states, causal, between):
    rows = qx.shape[0]
    assert rows == 2 * HG_SPAN
    qs = _silu(qx) * (HG_HEAD_DIM ** -0.5)
    fg = lb + (1.0 - lb) * jax.nn.sigmoid(fx)
    kk = 1.0 - fg
    lf = jnp.log(fg)
    between(0)
    sp, ch = HG_SPAN, HG_CHUNK
    g = [_cumsum_rows(lf[:sp]), _cumsum_rows(lf[sp:])]
    cat = lambda parts: jnp.concatenate(parts, axis=0)
    e_in = jnp.exp(cat(g))
    e_out = jnp.exp(cat([gp[sp - 1:] - gp for gp in g]))
    d_q = jnp.exp(cat([gp - gp[ch - 1:ch] for gp in g]))
    d_k = jnp.exp(cat([gp[ch - 1:ch] - gp for gp in g]))
    dec0, dec1 = e_in[sp - 1:sp], e_in[rows - 1:rows]
    q_in, k_out = qs * e_in, kk * e_out
    q_dg, k_dg = (qs * d_q).astype(BF16), (kk * d_k).astype(BF16)
    q_off, k_off = q_in[sp:].astype(BF16), k_out[:sp].astype(BF16)
    q_st = cat([q_in[:sp], q_in[sp:] * dec0]).astype(BF16)
    k_st = cat([k_out[:sp] * dec1, k_out[sp:]]).astype(BF16)
    dec = dec0 * dec1
    between(1)
    heads = range(len(states))
    lanes = [slice(h * HG_HEAD_DIM, (h + 1) * HG_HEAD_DIM) for h in heads]
    attn = []
    for h in heads:
        sl = lanes[h]
        a_dg = lax.dot_general(q_dg[:, sl], k_dg[:, sl], NT_DIMS, preferred_element_type=F32)
        a_dg = jnp.where(causal, a_dg, 0.0)
        a_off = lax.dot_general(q_off[:, sl], k_off[:, sl], NT_DIMS, preferred_element_type=F32)
        attn.append(cat([a_dg[:sp], jnp.concatenate([a_off, a_dg[sp:, sp:]], axis=1)]).astype(BF16))
    between(2)
    outs, new_states = [], []
    for h in heads:
        sl, st = lanes[h], states[h]
        o_h = (jnp.dot(attn[h], vx[:, sl], preferred_element_type=F32)
               + lax.dot_general(q_st[:, sl], st.astype(BF16), NT_DIMS, preferred_element_type=F32))
        new_states.append(st * dec[:, sl] + lax.dot_general(vx[:, sl], k_st[:, sl], TN_DIMS,
                                                            preferred_element_type=F32))
        ms = jnp.mean(o_h * o_h, axis=-1, keepdims=True)
        outs.append(o_h * lax.rsqrt(ms + EPS))
    return jnp.concatenate(outs, axis=1), new_states


def _hgrn_layer_kernel(h_ref, wq_ref, wf_ref, wi_ref, wg_ref, wcq_ref, wcg_ref,
                       lb_ref, ng_ref, k_ref, v_ref, *rest, layer, sub, n_mix_steps, side_blocks):
    n_side = len(side_blocks)
    side_src, (bm_ref, bc_ref), rest = rest[:n_side], rest[n_side:n_side + 2], rest[n_side + 2:]
    side_dst, (st_ref,) = rest[:n_side], rest[n_side:]
    s = pl.program_id(1)
    j = pl.program_id(2)
    tm = h_ref.shape[0]
    n_sub = tm // sub
    heads = bm_ref.shape[1] // HG_HEAD_DIM
    _side_cast(_linear_step(), side_src, side_dst, side_blocks)

    def project(r, w_ref):
        return jnp.dot(h_ref[r * sub:(r + 1) * sub, :], w_ref[...], preferred_element_type=F32)

    @pl.when((j == 0) & (s == 0))
    def _():
        st_ref[...] = jnp.zeros_like(st_ref)

    @pl.when(j < n_mix_steps)
    def _():
        sm = jax.nn.softmax(lb_ref[...], axis=0)
        lb = jnp.sum(sm[:layer + 1], axis=0, keepdims=True) - sm[0:1]
        r_i = lax.broadcasted_iota(jnp.int32, (sub, sub), 0)
        c_i = lax.broadcasted_iota(jnp.int32, (sub, sub), 1)
        causal = (r_i // HG_SPAN == c_i // HG_SPAN) & (c_i <= r_i)
        states = [st_ref[j * heads + h] for h in range(heads)]
        w_refs = (wq_ref, wf_ref, wi_ref, wg_ref)
        cur = [project(0, w) for w in w_refs]
        for r in range(n_sub):
            nxt = [None] * len(w_refs)

            def between(k, r=r, nxt=nxt):
                if r + 1 < n_sub:
                    nxt[k] = project(r + 1, w_refs[k])

            qx, fx, ix, gx = cur
            o, states = _hgrn_rows(qx, fx, ix.astype(BF16), lb, states, causal, between)
            between(3)
            bm_ref[r * sub:(r + 1) * sub, :] = ((o * ng_ref[...]) * _silu(gx)).astype(BF16)
            cur = nxt
        for h in range(heads):
            st_ref[j * heads + h] = states[h]

    @pl.when(j >= n_mix_steps)
    def _():
        _attention_step(project, wcq_ref, wcg_ref, k_ref, v_ref, bc_ref, n_sub, sub)


def _hgrn_layer(h, w_in, lb_all, norm_g, kv, side, *, kv_col0, layer, batch, seq, n_mem, tm,
                sub):
    M, D = h.shape
    heads_per_step, ca_heads_per_step = 2, 2
    W = heads_per_step * HG_HEAD_DIM
    wc = ca_heads_per_step * CA_HEAD_DIM
    assert seq % tm == 0 and tm % sub == 0 and sub == 2 * HG_SPAN
    n_mix, n_ca = E_MIX // W, E_CA // wc
    spt = seq // tm
    mix_blk = lambda j: jnp.minimum(j, n_mix - 1)
    ca_blk = lambda j: jnp.clip(j - n_mix, 0, n_ca - 1)
    gate0 = 3 * E_MIX + E_CA
    assert gate0 % W == 0 and (3 * E_MIX) % wc == 0 and (gate0 + E_MIX) % wc == 0
    mix_w = lambda col0: pl.BlockSpec((D, W), lambda b, s, j: (0, col0 // W + mix_blk(j)))
    ca_w = lambda col0: pl.BlockSpec((D, wc), lambda b, s, j: (0, col0 // wc + ca_blk(j)))
    blocks = ([((tm, D), BF16)] + 4 * [((D, W), BF16)] + 2 * [((D, wc), BF16)]
              + [((DEPTH, W), F32), ((1, W), F32), ((n_mem, wc), BF16), ((n_mem, wc), BF16),
                 ((tm, W), BF16), ((tm, wc), BF16)])
    scratch = [((E_MIX // HG_HEAD_DIM, HG_HEAD_DIM, HG_HEAD_DIM), F32)]
    n_steps = n_mix + n_ca
    side_in, side_out, side_shapes, side_blk, side_blocks = _side_cast_plan(
        side, lambda b, s, j: (b * spt + s) * n_steps + j, batch * spt * n_steps)
    return pl.pallas_call(
        functools.partial(_hgrn_layer_kernel, layer=layer, sub=sub, n_mix_steps=n_mix,
                          side_blocks=side_blocks),
        grid=(batch, spt, n_steps),
        in_specs=[pl.BlockSpec((tm, D), lambda b, s, j: (b * spt + s, 0)),
                  mix_w(0), mix_w(E_MIX), mix_w(2 * E_MIX), mix_w(gate0),
                  ca_w(3 * E_MIX), ca_w(gate0 + E_MIX),
                  pl.BlockSpec((DEPTH, W), lambda b, s, j: (0, mix_blk(j))),
                  pl.BlockSpec((1, W), lambda b, s, j: (0, mix_blk(j))),
                  pl.BlockSpec((n_mem, wc), lambda b, s, j: (b, kv_col0 // wc + ca_blk(j))),
                  pl.BlockSpec((n_mem, wc),
                               lambda b, s, j: (b, (kv_col0 + E_CA) // wc + ca_blk(j)))] + side_in,
        out_specs=[pl.BlockSpec((tm, W), lambda b, s, j: (b * spt + s, mix_blk(j))),
                   pl.BlockSpec((tm, wc), lambda b, s, j: (b * spt + s, ca_blk(j)))] + side_out,
        out_shape=[jax.ShapeDtypeStruct((M, E_MIX), BF16),
                   jax.ShapeDtypeStruct((M, E_CA), BF16)] + side_shapes,
        scratch_shapes=[pltpu.VMEM(s_, d_) for s_, d_ in scratch],
        compiler_params=_compiler_params(("arbitrary", "arbitrary", "arbitrary"),
                                         blocks + side_blk, scratch,
                                         temps=32 * _nbytes((sub, W), F32)),
        name="hgrn_layer",
    )(h, w_in, w_in, w_in, w_in, w_in, w_in, lb_all, norm_g.reshape(1, E_MIX), kv, kv,
      *[a for a, _, _ in side])


def _out_proj_kernel(x_ref, bm_ref, bc_ref, wm_ref, wc_ref, g_ref, *out_refs, last):
    acc = jnp.dot(bm_ref[...], wm_ref[...], preferred_element_type=F32)
    acc = acc + jnp.dot(bc_ref[...], wc_ref[...], preferred_element_type=F32)
    y = x_ref[...] + acc
    normed = _rms_rows(y, g_ref[...])
    if last:
        out_refs[0][...] = normed
    else:
        out_refs[0][...] = y
        out_refs[1][...] = normed.astype(BF16)


def _out_proj(x, bm, bc, w_out, g, *, tm, last):
    M, D = x.shape
    assert M % tm == 0
    resident = pl.Buffered(1)
    row_blk = lambda w: pl.BlockSpec((tm, w), lambda i: (i, 0))
    out_specs, out_shape = [row_blk(D)], [jax.ShapeDtypeStruct((M, D), F32)]
    blocks = [((tm, D), F32), ((tm, E_MIX), BF16), ((tm, E_CA), BF16), ((1, D), F32), ((tm, D), F32)]
    if not last:
        out_specs.append(row_blk(D))
        out_shape.append(jax.ShapeDtypeStruct((M, D), BF16))
        blocks.append(((tm, D), BF16))
    single = [((E_MIX, D), BF16), ((E_CA, D), BF16)]
    return pl.pallas_call(
        functools.partial(_out_proj_kernel, last=last),
        grid=(M // tm,),
        in_specs=[row_blk(D), row_blk(E_MIX), row_blk(E_CA),
                  pl.BlockSpec((E_MIX, D), lambda i: (0, 0), pipeline_mode=resident),
                  pl.BlockSpec((E_CA, D), lambda i: (E_MIX // E_CA, 0), pipeline_mode=resident),
                  pl.BlockSpec((1, D), lambda i: (0, 0))],
        out_specs=out_specs,
        out_shape=out_shape,
        compiler_params=_compiler_params(("parallel",), blocks, single,
                                         temps=2 * _nbytes((tm, D), F32)),
        name="out_proj_last" if last else "out_proj",
    )(x, bm, bc, w_out, w_out, g.reshape(1, D))


def kernel(x, mem, norm_g, mem_norm_g, w_kv, w_out, pool_w_in, pool_w_grp, pool_scale,
           hgrn_w_in, hgrn_lb, hgrn_norm_g, final_g):
    batch, seq, d_model = x.shape
    n_mem = mem.shape[1]
    x2 = x.reshape(batch * seq, d_model)
    mem2 = mem.reshape(batch * n_mem, d_model)
    assert DEPTH == 2 and N_MIXERS == 2
    kv = _kv_proj(mem2, mem_norm_g, w_kv, tm=512, tn=1024)
    common = dict(batch=batch, seq=seq, n_mem=n_mem, sub=256)
    bm, bc, w_hg, w_o0 = _pool_layer(
        x2, norm_g[0], pool_w_in[0], pool_w_grp[0], pool_scale[0], kv,
        [(hgrn_w_in, 0, 32), (w_out, 0, 64)], kv_col0=0, tm=1024, **common)
    x2, h = _out_proj(x2, bm, bc, w_o0, norm_g[1], tm=512, last=False)
    bm, bc, w_o1 = _hgrn_layer(h, w_hg, hgrn_lb, hgrn_norm_g[0], kv, [(w_out, 1, 64)],
                               kv_col0=2 * E_CA, layer=1, tm=2048, **common)
    y, = _out_proj(x2, bm, bc, w_o1, final_g, tm=512, last=True)
    return y.reshape(batch, seq, d_model)
```

```python
import functools

import jax
import jax.numpy as jnp
from jax import lax
from jax.experimental import pallas as pl
from jax.experimental.pallas import tpu as pltpu

F32 = jnp.float32
BF16 = jnp.bfloat16

DEPTH = 2
N_MIXERS = 2
E_MIX = 3072
E_CA = 1024
E_BRANCH = E_MIX + E_CA
CA_HEADS = 4
CA_HEAD_DIM = E_CA // CA_HEADS
POOL_WINDOWS = (2, 4, 8, 16)
POOL_GROUP = E_MIX // len(POOL_WINDOWS)
POOL_HALO = max(POOL_WINDOWS)
HG_HEAD_DIM = 128
HG_CHUNK = 64
HG_SPAN = 2 * HG_CHUNK
EPS = 1e-6

V7X_VMEM_BYTES = 64 * 1024 * 1024
F32_SUBLANES = 8
V7X_MXU_WIDTH = 256
NT_DIMS = (((1,), (1,)), ((), ()))
TN_DIMS = (((0,), (0,)), ((), ()))


def _nbytes(shape, dtype):
    n = 1
    for s in shape:
        n *= s
    return n * jnp.dtype(dtype).itemsize


def _compiler_params(semantics, blocks, scratch=(), temps=0):
    need = 2 * sum(_nbytes(s, d) for s, d in blocks) + sum(_nbytes(s, d) for s, d in scratch) + temps
    limit = min(need + need // 4 + (2 << 20), V7X_VMEM_BYTES - (4 << 20))
    return pltpu.CompilerParams(dimension_semantics=semantics, vmem_limit_bytes=int(limit))


def _silu(x):
    return x * jax.nn.sigmoid(x)


def _rms_rows(x, g):
    ms = jnp.mean(x * x, axis=-1, keepdims=True)
    return (x * lax.rsqrt(ms + EPS)) * g


def _norm_rows_to(h_ref, x_ref, g_ref, row_chunk):
    def body(r, carry):
        rows = pl.ds(pl.multiple_of(r * row_chunk, row_chunk), row_chunk)
        h_ref[rows, :] = _rms_rows(x_ref[rows, :], g_ref[...]).astype(BF16)
        return carry
    lax.fori_loop(0, x_ref.shape[0] // row_chunk, body, 0)


def _attend(q, k, v):
    s = lax.dot_general(q, k, NT_DIMS, preferred_element_type=F32) * (CA_HEAD_DIM ** -0.5)
    e = jnp.exp(s - jnp.max(s, axis=-1, keepdims=True))
    p = e / jnp.sum(e, axis=-1, keepdims=True)
    return jnp.dot(p.astype(BF16), v, preferred_element_type=F32)


def _side_cast_plan(srcs, step_of, n_steps):
    in_specs, out_specs, out_shapes, blocks, n_blocks = [], [], [], [], []
    for arr, lead, rows in srcs:
        R, N = arr.shape[-2:]
        nb = R // rows
        assert R % rows == 0 and nb <= n_steps
        blk = lambda *ids, nb=nb: jnp.minimum(step_of(*ids), nb - 1)
        if lead is None:
            in_specs.append(pl.BlockSpec((rows, N), lambda *ids, blk=blk: (blk(*ids), 0)))
        else:
            in_specs.append(pl.BlockSpec((None, rows, N),
                                         lambda *ids, blk=blk, lead=lead: (lead, blk(*ids), 0)))
        out_specs.append(pl.BlockSpec((rows, N), lambda *ids, blk=blk: (blk(*ids), 0)))
        out_shapes.append(jax.ShapeDtypeStruct((R, N), BF16))
        blocks += [((rows, N), arr.dtype), ((rows, N), BF16)]
        n_blocks.append(nb)
    return in_specs, out_specs, out_shapes, blocks, tuple(n_blocks)


def _side_cast(src_refs, dst_refs):
    for src, dst in zip(src_refs, dst_refs):
        dst[...] = src[...].astype(BF16)


def _kv_proj_kernel(x_ref, g_ref, w_ref, *rest, row_chunk, n_side):
    side_src, o_ref, side_dst, h_ref = (rest[:n_side], rest[n_side], rest[n_side + 1:-1],
                                        rest[-1])

    @pl.when((pl.program_id(1) == 0) & (pl.program_id(2) == 0))
    def _():
        _norm_rows_to(h_ref, x_ref, g_ref, row_chunk)

    _side_cast(side_src, side_dst)
    acc = jnp.dot(h_ref[...], w_ref[...].astype(BF16), preferred_element_type=F32)
    o_ref[...] = acc.astype(o_ref.dtype)


def _kv_proj(mem, g, w_kv, side, *, tm, tn):
    M, D = mem.shape
    L, _, N = w_kv.shape
    assert M % tm == 0 and N % tn == 0
    blocks = [((tm, D), F32), ((1, D), F32), ((D, tn), w_kv.dtype), ((tm, tn), BF16)]
    grid = (M // tm, L, N // tn)
    side_in, side_out, side_shapes, side_blk, _ = _side_cast_plan(
        side, lambda i, l, j: (i * grid[1] + l) * grid[2] + j, grid[0] * grid[1] * grid[2])
    return pl.pallas_call(
        functools.partial(_kv_proj_kernel, row_chunk=128, n_side=len(side)),
        grid=grid,
        in_specs=[pl.BlockSpec((tm, D), lambda i, l, j: (i, 0)),
                  pl.BlockSpec((1, D), lambda i, l, j: (0, 0)),
                  pl.BlockSpec((None, D, tn), lambda i, l, j: (l, 0, j))] + side_in,
        out_specs=[pl.BlockSpec((tm, tn), lambda i, l, j: (i, l * (N // tn) + j))] + side_out,
        out_shape=[jax.ShapeDtypeStruct((M, L * N), BF16)] + side_shapes,
        scratch_shapes=[pltpu.VMEM((tm, D), BF16)],
        compiler_params=_compiler_params(("arbitrary", "arbitrary", "arbitrary"),
                                         blocks + side_blk, [((tm, D), BF16)],
                                         temps=2 * _nbytes((tm, tn), F32)),
        name="kv_proj",
    )(mem, g.reshape(1, D), w_kv, *[a for a, _, _ in side])


def _attention_step(project, wcq_ref, wcg_ref, k_ref, v_ref, bc_ref, n_sub, sub):
    for r in range(n_sub):
        q = project(r, wcq_ref).astype(BF16)
        gate = project(r, wcg_ref)
        for hh in range(bc_ref.shape[1] // CA_HEAD_DIM):
            sl = slice(hh * CA_HEAD_DIM, (hh + 1) * CA_HEAD_DIM)
            o = _attend(q[:, sl], k_ref[:, sl], v_ref[:, sl])
            bc_ref[r * sub:(r + 1) * sub, sl] = (o * _silu(gate[:, sl])).astype(BF16)


def _pool_rows(u, halo, window, pos0):
    s = jnp.concatenate([halo, u], axis=0)
    k = 1
    while k < POOL_HALO:
        s = jnp.where(k < window, s + pltpu.roll(s, k, axis=0), s)
        k *= 2
    pos = pos0 + lax.broadcasted_iota(jnp.int32, u.shape, 0)
    cnt = jnp.minimum(pos + 1, window).astype(F32)
    return s[POOL_HALO:, :] / cnt - u


def _pool_layer_kernel(x_ref, g_ref, *rest, sub, n_mix_steps, row_chunk, n_piece, n_side):
    wu_refs, wg_refs, rest = rest[:n_piece], rest[n_piece:2 * n_piece], rest[2 * n_piece:]
    (wcq_ref, wcg_ref, wgrp_ref, scale_ref, k_ref, v_ref), rest = rest[:6], rest[6:]
    side_src, (bm_ref, bc_ref), rest = rest[:n_side], rest[n_side:n_side + 2], rest[n_side + 2:]
    side_dst, (h_ref, halo_ref) = rest[:n_side], rest[n_side:]
    s = pl.program_id(1)
    j = pl.program_id(2)
    tm = x_ref.shape[0]
    n_sub = tm // sub

    def project(r, w_ref):
        return jnp.dot(h_ref[r * sub:(r + 1) * sub, :], w_ref[...], preferred_element_type=F32)

    def project_pieces(r, w_refs):
        return jnp.concatenate([project(r, w) for w in w_refs], axis=1)

    @pl.when(j == 0)
    def _():
        _norm_rows_to(h_ref, x_ref, g_ref, row_chunk)

    @pl.when(j < n_mix_steps)
    def _():
        _side_cast(side_src, side_dst)
        window = jnp.int32(POOL_WINDOWS[0])
        for g in range(1, len(POOL_WINDOWS)):
            window = jnp.where(j == g, POOL_WINDOWS[g], window)
        halo = jnp.where(s == 0, 0.0, halo_ref[j])
        u, gate = project_pieces(0, wu_refs), project_pieces(0, wg_refs)
        for r in range(n_sub):
            pooled = _pool_rows(u, halo, window, s * tm + r * sub)
            halo = u[sub - POOL_HALO:, :]
            u_next = project_pieces(r + 1, wu_refs) if r + 1 < n_sub else None
            y = jnp.dot(pooled.astype(BF16), wgrp_ref[0], preferred_element_type=F32)
            gate_next = project_pieces(r + 1, wg_refs) if r + 1 < n_sub else None
            bm_ref[r * sub:(r + 1) * sub, :] = ((y * scale_ref[...]) * _silu(gate)).astype(BF16)
            u, gate = u_next, gate_next
        halo_ref[j] = halo

    @pl.when(j >= n_mix_steps)
    def _():
        _side_cast(side_src, side_dst)
        _attention_step(project, wcq_ref, wcg_ref, k_ref, v_ref, bc_ref, n_sub, sub)


def _pool_layer(x, g, w_in, w_grp, scale, kv, side, *, kv_col0, batch, seq, n_mem, tm, sub):
    M, D = x.shape
    assert all(w & (w - 1) == 0 for w in POOL_WINDOWS) and seq % tm == 0 and tm % sub == 0
    ca_heads_per_step = 2
    C = POOL_GROUP
    wc = ca_heads_per_step * CA_HEAD_DIM
    n_mix, n_ca = len(POOL_WINDOWS), E_CA // wc
    spt = seq // tm
    gate0 = E_MIX + E_CA
    piece = V7X_MXU_WIDTH
    n_piece = C // piece
    assert C % piece == 0 and gate0 % piece == 0 and E_MIX % wc == 0 and (gate0 + E_MIX) % wc == 0
    mix_blk = lambda j: jnp.minimum(j, n_mix - 1)
    ca_blk = lambda j: jnp.clip(j - n_mix, 0, n_ca - 1)
    mix_w = lambda col0: [
        pl.BlockSpec((D, piece), lambda b, s, j, p=p: (0, col0 // piece + n_piece * mix_blk(j) + p))
        for p in range(n_piece)]
    ca_w = lambda col0: pl.BlockSpec((D, wc), lambda b, s, j: (0, col0 // wc + ca_blk(j)))
    blocks = ([((tm, D), F32), ((1, D), F32)] + 2 * [((D, C), BF16)] + 2 * [((D, wc), BF16)]
              + [((1, C, C), BF16), ((1, C), F32), ((n_mem, wc), BF16), ((n_mem, wc), BF16),
                 ((tm, C), BF16), ((tm, wc), BF16)])
    scratch = [((tm, D), BF16), ((n_mix, POOL_HALO, C), F32)]
    n_steps = n_mix + n_ca
    side_in, side_out, side_shapes, side_blk, _ = _side_cast_plan(
        side, lambda b, s, j: (b * spt + s) * n_steps + j, batch * spt * n_steps)
    return pl.pallas_call(
        functools.partial(_pool_layer_kernel, sub=sub, n_mix_steps=n_mix, row_chunk=128,
                          n_piece=n_piece, n_side=len(side)),
        grid=(batch, spt, n_steps),
        in_specs=[pl.BlockSpec((tm, D), lambda b, s, j: (b * spt + s, 0)),
                  pl.BlockSpec((1, D), lambda b, s, j: (0, 0)),
                  *mix_w(0), *mix_w(gate0), ca_w(E_MIX), ca_w(gate0 + E_MIX),
                  pl.BlockSpec((1, C, C), lambda b, s, j: (mix_blk(j), 0, 0)),
                  pl.BlockSpec((1, C), lambda b, s, j: (0, mix_blk(j))),
                  pl.BlockSpec((n_mem, wc), lambda b, s, j: (b, kv_col0 // wc + ca_blk(j))),
                  pl.BlockSpec((n_mem, wc),
                               lambda b, s, j: (b, (kv_col0 + E_CA) // wc + ca_blk(j)))] + side_in,
        out_specs=[pl.BlockSpec((tm, C), lambda b, s, j: (b * spt + s, mix_blk(j))),
                   pl.BlockSpec((tm, wc), lambda b, s, j: (b * spt + s, ca_blk(j)))] + side_out,
        out_shape=[jax.ShapeDtypeStruct((M, E_MIX), BF16),
                   jax.ShapeDtypeStruct((M, E_CA), BF16)] + side_shapes,
        scratch_shapes=[pltpu.VMEM(s_, d_) for s_, d_ in scratch],
        compiler_params=_compiler_params(("arbitrary", "arbitrary", "arbitrary"),
                                         blocks + side_blk, scratch,
                                         temps=12 * _nbytes((sub + POOL_HALO, C), F32)),
        name="pool_layer",
    )(x, g.reshape(1, D), *(2 * n_piece + 2) * [w_in], w_grp.astype(BF16),
      scale.reshape(1, E_MIX), kv, kv, *[a for a, _, _ in side])


def _cumsum_rows(x):
    rows, width = x.shape
    r = lax.broadcasted_iota(jnp.int32, (F32_SUBLANES, width), 0)
    k = 1
    while k < F32_SUBLANES:
        sh = pltpu.roll(x, k, axis=0)
        top = jnp.where(r >= k, sh[:F32_SUBLANES], 0.0)
        x = x + jnp.concatenate([top, sh[F32_SUBLANES:]], axis=0)
        k *= 2
    while k < rows:
        x = jnp.concatenate([x[:k], x[k:] + x[:rows - k]], axis=0)
        k *= 2
    return x


def _hgrn_rows(qx, fx, vx, lb, states, causal, between):
    rows = qx.shape[0]
    assert rows == 2 * HG_SPAN
    qs = _silu(qx) * (HG_HEAD_DIM ** -0.5)
    fg = lb + (1.0 - lb) * jax.nn.sigmoid(fx)
    kk = 1.0 - fg
    lf = jnp.log(fg)
    between(0)
    sp, ch = HG_SPAN, HG_CHUNK
    g = [_cumsum_rows(lf[:sp]), _cumsum_rows(lf[sp:])]
    cat = lambda parts: jnp.concatenate(parts, axis=0)
    e_in = jnp.exp(cat(g))
    e_out = jnp.exp(cat([gp[sp - 1:] - gp for gp in g]))
    d_q = jnp.exp(cat([gp - gp[ch - 1:ch] for gp in g]))
    d_k = jnp.exp(cat([gp[ch - 1:ch] - gp for gp in g]))
    dec0, dec1 = e_in[sp - 1:sp], e_in[rows - 1:rows]
    q_in, k_out = qs * e_in, kk * e_out
    q_dg, k_dg = (qs * d_q).astype(BF16), (kk * d_k).astype(BF16)
    q_off, k_off = q_in[sp:].astype(BF16), k_out[:sp].astype(BF16)
    q_st = cat([q_in[:sp], q_in[sp:] * dec0]).astype(BF16)
    k_st = cat([k_out[:sp] * dec1, k_out[sp:]]).astype(BF16)
    dec = dec0 * dec1
    between(1)
    heads = range(len(states))
    lanes = [slice(h * HG_HEAD_DIM, (h + 1) * HG_HEAD_DIM) for h in heads]
    attn = []
    for h in heads:
        sl = lanes[h]
        a_dg = lax.dot_general(q_dg[:, sl], k_dg[:, sl], NT_DIMS, preferred_element_type=F32)
        a_dg = jnp.where(causal, a_dg, 0.0)
        a_off = lax.dot_general(q_off[:, sl], k_off[:, sl], NT_DIMS, preferred_element_type=F32)
        attn.append(cat([a_dg[:sp], jnp.concatenate([a_off, a_dg[sp:, sp:]], axis=1)]).astype(BF16))
    between(2)
    outs, new_states = [], []
    for h in heads:
        sl, st = lanes[h], states[h]
        o_h = (jnp.dot(attn[h], vx[:, sl], preferred_element_type=F32)
               + lax.dot_general(q_st[:, sl], st.astype(BF16), NT_DIMS, preferred_element_type=F32))
        new_states.append(st * dec[:, sl] + lax.dot_general(vx[:, sl], k_st[:, sl], TN_DIMS,
                                                            preferred_element_type=F32))
        ms = jnp.mean(o_h * o_h, axis=-1, keepdims=True)
        outs.append(o_h * lax.rsqrt(ms + EPS))
    return jnp.concatenate(outs, axis=1), new_states


def _hgrn_layer_kernel(h_ref, wq_ref, wf_ref, wi_ref, wg_ref, wcq_ref, wcg_ref,
                       lb_ref, ng_ref, k_ref, v_ref, *rest, layer, sub, n_mix_steps, n_side):
    side_src, (bm_ref, bc_ref), rest = rest[:n_side], rest[n_side:n_side + 2], rest[n_side + 2:]
    side_dst, (st_ref,) = rest[:n_side], rest[n_side:]
    s = pl.program_id(1)
    j = pl.program_id(2)
    tm = h_ref.shape[0]
    n_sub = tm // sub
    heads = bm_ref.shape[1] // HG_HEAD_DIM

    def project(r, w_ref):
        return jnp.dot(h_ref[r * sub:(r + 1) * sub, :], w_ref[...], preferred_element_type=F32)

    @pl.when((j == 0) & (s == 0))
    def _():
        st_ref[...] = jnp.zeros_like(st_ref)

    @pl.when(j < n_mix_steps)
    def _():
        _side_cast(side_src, side_dst)
        sm = jax.nn.softmax(lb_ref[...], axis=0)
        lb = jnp.sum(sm[:layer + 1], axis=0, keepdims=True) - sm[0:1]
        r_i = lax.broadcasted_iota(jnp.int32, (sub, sub), 0)
        c_i = lax.broadcasted_iota(jnp.int32, (sub, sub), 1)
        causal = (r_i // HG_SPAN == c_i // HG_SPAN) & (c_i <= r_i)
        states = [st_ref[j * heads + h] for h in range(heads)]
        w_refs = (wq_ref, wf_ref, wi_ref, wg_ref)
        cur = [project(0, w) for w in w_refs]
        for r in range(n_sub):
            nxt = [None] * len(w_refs)

            def between(k, r=r, nxt=nxt):
                if r + 1 < n_sub:
                    nxt[k] = project(r + 1, w_refs[k])

            qx, fx, ix, gx = cur
            o, states = _hgrn_rows(qx, fx, ix.astype(BF16), lb, states, causal, between)
            between(3)
            bm_ref[r * sub:(r + 1) * sub, :] = ((o * ng_ref[...]) * _silu(gx)).astype(BF16)
            cur = nxt
        for h in range(heads):
            st_ref[j * heads + h] = states[h]

    @pl.when(j >= n_mix_steps)
    def _():
        _side_cast(side_src, side_dst)
        _attention_step(project, wcq_ref, wcg_ref, k_ref, v_ref, bc_ref, n_sub, sub)


def _hgrn_layer(h, w_in, lb_all, norm_g, kv, side, *, kv_col0, layer, batch, seq, n_mem, tm,
                sub):
    M, D = h.shape
    heads_per_step, ca_heads_per_step = 2, 2
    W = heads_per_step * HG_HEAD_DIM
    wc = ca_heads_per_step * CA_HEAD_DIM
    assert seq % tm == 0 and tm % sub == 0 and sub == 2 * HG_SPAN
    n_mix, n_ca = E_MIX // W, E_CA // wc
    spt = seq // tm
    mix_blk = lambda j: jnp.minimum(j, n_mix - 1)
    ca_blk = lambda j: jnp.clip(j - n_mix, 0, n_ca - 1)
    gate0 = 3 * E_MIX + E_CA
    assert gate0 % W == 0 and (3 * E_MIX) % wc == 0 and (gate0 + E_MIX) % wc == 0
    mix_w = lambda col0: pl.BlockSpec((D, W), lambda b, s, j: (0, col0 // W + mix_blk(j)))
    ca_w = lambda col0: pl.BlockSpec((D, wc), lambda b, s, j: (0, col0 // wc + ca_blk(j)))
    blocks = ([((tm, D), BF16)] + 4 * [((D, W), BF16)] + 2 * [((D, wc), BF16)]
              + [((DEPTH, W), F32), ((1, W), F32), ((n_mem, wc), BF16), ((n_mem, wc), BF16),
                 ((tm, W), BF16), ((tm, wc), BF16)])
    scratch = [((E_MIX // HG_HEAD_DIM, HG_HEAD_DIM, HG_HEAD_DIM), F32)]
    n_steps = n_mix + n_ca
    side_in, side_out, side_shapes, side_blk, _ = _side_cast_plan(
        side, lambda b, s, j: (b * spt + s) * n_steps + j, batch * spt * n_steps)
    return pl.pallas_call(
        functools.partial(_hgrn_layer_kernel, layer=layer, sub=sub, n_mix_steps=n_mix,
                          n_side=len(side)),
        grid=(batch, spt, n_steps),
        in_specs=[pl.BlockSpec((tm, D), lambda b, s, j: (b * spt + s, 0)),
                  mix_w(0), mix_w(E_MIX), mix_w(2 * E_MIX), mix_w(gate0),
                  ca_w(3 * E_MIX), ca_w(gate0 + E_MIX),
                  pl.BlockSpec((DEPTH, W), lambda b, s, j: (0, mix_blk(j))),
                  pl.BlockSpec((1, W), lambda b, s, j: (0, mix_blk(j))),
                  pl.BlockSpec((n_mem, wc), lambda b, s, j: (b, kv_col0 // wc + ca_blk(j))),
                  pl.BlockSpec((n_mem, wc),
                               lambda b, s, j: (b, (kv_col0 + E_CA) // wc + ca_blk(j)))] + side_in,
        out_specs=[pl.BlockSpec((tm, W), lambda b, s, j: (b * spt + s, mix_blk(j))),
                   pl.BlockSpec((tm, wc), lambda b, s, j: (b * spt + s, ca_blk(j)))] + side_out,
        out_shape=[jax.ShapeDtypeStruct((M, E_MIX), BF16),
                   jax.ShapeDtypeStruct((M, E_CA), BF16)] + side_shapes,
        scratch_shapes=[pltpu.VMEM(s_, d_) for s_, d_ in scratch],
        compiler_params=_compiler_params(("arbitrary", "arbitrary", "arbitrary"),
                                         blocks + side_blk, scratch,
                                         temps=32 * _nbytes((sub, W), F32)),
        name="hgrn_layer",
    )(h, w_in, w_in, w_in, w_in, w_in, w_in, lb_all, norm_g.reshape(1, E_MIX), kv, kv,
      *[a for a, _, _ in side])


def _out_proj_kernel(x_ref, bm_ref, bc_ref, wm_ref, wc_ref, g_ref, *out_refs, last):
    acc = jnp.dot(bm_ref[...], wm_ref[...], preferred_element_type=F32)
    acc = acc + jnp.dot(bc_ref[...], wc_ref[...], preferred_element_type=F32)
    y = x_ref[...] + acc
    normed = _rms_rows(y, g_ref[...])
    if last:
        out_refs[0][...] = normed
    else:
        out_refs[0][...] = y
        out_refs[1][...] = normed.astype(BF16)


def _out_proj(x, bm, bc, w_out, g, *, tm, last):
    M, D = x.shape
    assert M % tm == 0
    resident = pl.Buffered(1)
    row_blk = lambda w: pl.BlockSpec((tm, w), lambda i: (i, 0))
    out_specs, out_shape = [row_blk(D)], [jax.ShapeDtypeStruct((M, D), F32)]
    blocks = [((tm, D), F32), ((tm, E_MIX), BF16), ((tm, E_CA), BF16), ((1, D), F32), ((tm, D), F32)]
    if not last:
        out_specs.append(row_blk(D))
        out_shape.append(jax.ShapeDtypeStruct((M, D), BF16))
        blocks.append(((tm, D), BF16))
    single = [((E_MIX, D), BF16), ((E_CA, D), BF16)]
    return pl.pallas_call(
        functools.partial(_out_proj_kernel, last=last),
        grid=(M // tm,),
        in_specs=[row_blk(D), row_blk(E_MIX), row_blk(E_CA),
                  pl.BlockSpec((E_MIX, D), lambda i: (0, 0), pipeline_mode=resident),
                  pl.BlockSpec((E_CA, D), lambda i: (E_MIX // E_CA, 0), pipeline_mode=resident),
                  pl.BlockSpec((1, D), lambda i: (0, 0))],
        out_specs=out_specs,
        out_shape=out_shape,
        compiler_params=_compiler_params(("parallel",), blocks, single,
                                         temps=2 * _nbytes((tm, D), F32)),
        name="out_proj_last" if last else "out_proj",
    )(x, bm, bc, w_out, w_out, g.reshape(1, D))


def kernel(x, mem, norm_g, mem_norm_g, w_kv, w_out, pool_w_in, pool_w_grp, pool_scale,
           hgrn_w_in, hgrn_lb, hgrn_norm_g, final_g):
    batch, seq, d_model = x.shape
    n_mem = mem.shape[1]
    x2 = x.reshape(batch * seq, d_model)
    mem2 = mem.reshape(batch * n_mem, d_model)
    assert DEPTH == 2 and N_MIXERS == 2
    kv, w_pool = _kv_proj(mem2, mem_norm_g, w_kv, [(pool_w_in, 0, 256)], tm=512, tn=1024)
    common = dict(batch=batch, seq=seq, n_mem=n_mem, sub=256)
    bm, bc, w_hg, w_o0 = _pool_layer(
        x2, norm_g[0], w_pool, pool_w_grp[0], pool_scale[0], kv,
        [(hgrn_w_in, 0, 32), (w_out, 0, 64)], kv_col0=0, tm=1024, **common)
    x2, h = _out_proj(x2, bm, bc, w_o0, norm_g[1], tm=512, last=False)
    bm, bc, w_o1 = _hgrn_layer(h, w_hg, hgrn_lb, hgrn_norm_g[0], kv, [(w_out, 1, 64)],
                               kv_col0=2 * E_CA, layer=1, tm=2048, **common)
    y, = _out_proj(x2, bm, bc, w_o1, final_g, tm=512, last=True)
    return y.reshape(batch, seq, d_model)
```

```python
import functools

import jax
import jax.numpy as jnp
from jax import lax
from jax.experimental import pallas as pl
from jax.experimental.pallas import tpu as pltpu

F32 = jnp.float32
BF16 = jnp.bfloat16

DEPTH = 2
N_MIXERS = 2
E_MIX = 3072
E_CA = 1024
E_BRANCH = E_MIX + E_CA
CA_HEADS = 4
CA_HEAD_DIM = E_CA // CA_HEADS
POOL_WINDOWS = (2, 4, 8, 16)
POOL_GROUP = E_MIX // len(POOL_WINDOWS)
POOL_HALO = max(POOL_WINDOWS)
HG_HEAD_DIM = 128
HG_CHUNK = 64
HG_SPAN = 2 * HG_CHUNK
EPS = 1e-6

V7X_VMEM_BYTES = 64 * 1024 * 1024
F32_SUBLANES = 8
V7X_MXU_WIDTH = 256
NT_DIMS = (((1,), (1,)), ((), ()))
TN_DIMS = (((0,), (0,)), ((), ()))


def _nbytes(shape, dtype):
    n = 1
    for s in shape:
        n *= s
    return n * jnp.dtype(dtype).itemsize


def _compiler_params(semantics, blocks, scratch=(), temps=0):
    need = 2 * sum(_nbytes(s, d) for s, d in blocks) + sum(_nbytes(s, d) for s, d in scratch) + temps
    limit = min(need + need // 4 + (2 << 20), V7X_VMEM_BYTES - (4 << 20))
    return pltpu.CompilerParams(dimension_semantics=semantics, vmem_limit_bytes=int(limit))


def _silu(x):
    return x * jax.nn.sigmoid(x)


def _rms_rows(x, g):
    ms = jnp.mean(x * x, axis=-1, keepdims=True)
    return (x * lax.rsqrt(ms + EPS)) * g


def _norm_rows_to(h_ref, x_ref, g_ref, row_chunk):
    def body(r, carry):
        rows = pl.ds(pl.multiple_of(r * row_chunk, row_chunk), row_chunk)
        h_ref[rows, :] = _rms_rows(x_ref[rows, :], g_ref[...]).astype(BF16)
        return carry
    lax.fori_loop(0, x_ref.shape[0] // row_chunk, body, 0)


def _attend(q, k, v):
    s = lax.dot_general(q, k, NT_DIMS, preferred_element_type=F32) * (CA_HEAD_DIM ** -0.5)
    e = jnp.exp(s - jnp.max(s, axis=-1, keepdims=True))
    p = e / jnp.sum(e, axis=-1, keepdims=True)
    return jnp.dot(p.astype(BF16), v, preferred_element_type=F32)


def _step_major_columns(mix_parts, mix_width, n_mix, ca_parts, ca_width, n_ca):
    order = []
    for j in range(n_mix):
        order += [(col0 + j * mix_width, mix_width) for col0 in mix_parts]
    for c in range(n_ca):
        order += [(col0 + c * ca_width, ca_width) for col0 in ca_parts]
    return tuple(order)


def _side_cast_plan(srcs, step_of, n_steps):
    in_specs, out_specs, out_shapes, blocks = [], [], [], []
    for arr, lead, rows, columns in srcs:
        R, N = arr.shape[-2:]
        nb = R // rows
        assert R % rows == 0 and nb <= n_steps
        assert columns is None or sorted(columns) == sorted(set(columns)) and \
            sum(w for _, w in columns) == N
        blk = lambda *ids, nb=nb: jnp.minimum(step_of(*ids), nb - 1)
        if lead is None:
            in_specs.append(pl.BlockSpec((rows, N), lambda *ids, blk=blk: (blk(*ids), 0)))
        else:
            in_specs.append(pl.BlockSpec((None, rows, N),
                                         lambda *ids, blk=blk, lead=lead: (lead, blk(*ids), 0)))
        out_specs.append(pl.BlockSpec((rows, N), lambda *ids, blk=blk: (blk(*ids), 0)))
        out_shapes.append(jax.ShapeDtypeStruct((R, N), BF16))
        blocks += [((rows, N), arr.dtype), ((rows, N), BF16)]
    return in_specs, out_specs, out_shapes, blocks, tuple(c for _, _, _, c in srcs)


def _side_cast(src_refs, dst_refs, columns):
    for src, dst, cols in zip(src_refs, dst_refs, columns):
        if cols is None:
            dst[...] = src[...].astype(BF16)
        else:
            dst[...] = jnp.concatenate([src[:, a:a + w] for a, w in cols], axis=1).astype(BF16)


def _kv_proj_kernel(x_ref, g_ref, w_ref, *rest, row_chunk, side_columns):
    n_side = len(side_columns)
    side_src, o_ref, side_dst, h_ref = (rest[:n_side], rest[n_side], rest[n_side + 1:-1],
                                        rest[-1])

    @pl.when((pl.program_id(1) == 0) & (pl.program_id(2) == 0))
    def _():
        _norm_rows_to(h_ref, x_ref, g_ref, row_chunk)

    _side_cast(side_src, side_dst, side_columns)
    acc = jnp.dot(h_ref[...], w_ref[...].astype(BF16), preferred_element_type=F32)
    o_ref[...] = acc.astype(o_ref.dtype)


def _kv_proj(mem, g, w_kv, side, *, tm, tn):
    M, D = mem.shape
    L, _, N = w_kv.shape
    assert M % tm == 0 and N == 2 * E_CA and E_CA % tn == 0
    blocks = [((tm, D), F32), ((1, D), F32), ((D, tn), w_kv.dtype), ((tm, tn), BF16)]
    grid = (M // tm, L, N // tn)
    n_pair = E_CA // tn
    out_blk = lambda j: jnp.where(j < n_pair, 2 * j, 2 * (j - n_pair) + 1)
    side_in, side_out, side_shapes, side_blk, side_columns = _side_cast_plan(
        side, lambda i, l, j: (i * grid[1] + l) * grid[2] + j, grid[0] * grid[1] * grid[2])
    return pl.pallas_call(
        functools.partial(_kv_proj_kernel, row_chunk=128, side_columns=side_columns),
        grid=grid,
        in_specs=[pl.BlockSpec((tm, D), lambda i, l, j: (i, 0)),
                  pl.BlockSpec((1, D), lambda i, l, j: (0, 0)),
                  pl.BlockSpec((None, D, tn), lambda i, l, j: (l, 0, j))] + side_in,
        out_specs=[pl.BlockSpec((tm, tn),
                                lambda i, l, j: (i, l * (N // tn) + out_blk(j)))] + side_out,
        out_shape=[jax.ShapeDtypeStruct((M, L * N), BF16)] + side_shapes,
        scratch_shapes=[pltpu.VMEM((tm, D), BF16)],
        compiler_params=_compiler_params(("arbitrary", "arbitrary", "arbitrary"),
                                         blocks + side_blk, [((tm, D), BF16)],
                                         temps=2 * _nbytes((tm, tn), F32)),
        name="kv_proj",
    )(mem, g.reshape(1, D), w_kv, *[a for a, _, _, _ in side])


def _attention_step(project, w_ref, kv_ref, bc_ref, n_sub, sub):
    wc = bc_ref.shape[1]
    for r in range(n_sub):
        q = project(r, w_ref[:, :wc]).astype(BF16)
        gate = project(r, w_ref[:, wc:])
        for hh in range(wc // CA_HEAD_DIM):
            sl = slice(hh * CA_HEAD_DIM, (hh + 1) * CA_HEAD_DIM)
            o = _attend(q[:, sl], kv_ref[:, sl],
                        kv_ref[:, wc + hh * CA_HEAD_DIM:wc + (hh + 1) * CA_HEAD_DIM])
            bc_ref[r * sub:(r + 1) * sub, sl] = (o * _silu(gate[:, sl])).astype(BF16)


def _pool_rows(u, halo, window, pos0):
    s = jnp.concatenate([halo, u], axis=0)
    k = 1
    while k < POOL_HALO:
        s = jnp.where(k < window, s + pltpu.roll(s, k, axis=0), s)
        k *= 2
    pos = pos0 + lax.broadcasted_iota(jnp.int32, u.shape, 0)
    cnt = jnp.minimum(pos + 1, window).astype(F32)
    return s[POOL_HALO:, :] / cnt - u


def _pool_layer_kernel(x_ref, g_ref, wmix_ref, wca_ref, wgrp_ref, scale_ref, kv_ref, *rest,
                       sub, n_mix_steps, row_chunk, side_columns):
    n_side = len(side_columns)
    side_src, (bm_ref, bc_ref), rest = rest[:n_side], rest[n_side:n_side + 2], rest[n_side + 2:]
    side_dst, (h_ref, halo_ref) = rest[:n_side], rest[n_side:]
    s = pl.program_id(1)
    j = pl.program_id(2)
    tm = x_ref.shape[0]
    n_sub = tm // sub
    C = bm_ref.shape[1]

    def project(r, w):
        return jnp.dot(h_ref[r * sub:(r + 1) * sub, :], w, preferred_element_type=F32)

    @pl.when(j == 0)
    def _():
        _norm_rows_to(h_ref, x_ref, g_ref, row_chunk)

    @pl.when(j < n_mix_steps)
    def _():
        _side_cast(side_src, side_dst, side_columns)
        window = jnp.int32(POOL_WINDOWS[0])
        for g in range(1, len(POOL_WINDOWS)):
            window = jnp.where(j == g, POOL_WINDOWS[g], window)
        halo = jnp.where(s == 0, 0.0, halo_ref[j])
        u, gate = project(0, wmix_ref[:, :C]), project(0, wmix_ref[:, C:])
        for r in range(n_sub):
            pooled = _pool_rows(u, halo, window, s * tm + r * sub)
            halo = u[sub - POOL_HALO:, :]
            u_next = project(r + 1, wmix_ref[:, :C]) if r + 1 < n_sub else None
            y = jnp.dot(pooled.astype(BF16), wgrp_ref[0], preferred_element_type=F32)
            gate_next = project(r + 1, wmix_ref[:, C:]) if r + 1 < n_sub else None
            bm_ref[r * sub:(r + 1) * sub, :] = ((y * scale_ref[...]) * _silu(gate)).astype(BF16)
            u, gate = u_next, gate_next
        halo_ref[j] = halo

    @pl.when(j >= n_mix_steps)
    def _():
        _side_cast(side_src, side_dst, side_columns)
        _attention_step(project, wca_ref, kv_ref, bc_ref, n_sub, sub)


POOL_CA_HEADS_PER_STEP = 2


def _pool_weight_columns():
    wc = POOL_CA_HEADS_PER_STEP * CA_HEAD_DIM
    gate0 = E_MIX + E_CA
    return _step_major_columns((0, gate0), POOL_GROUP, len(POOL_WINDOWS),
                               (E_MIX, gate0 + E_MIX), wc, E_CA // wc)


def _pool_layer(x, g, w_in, w_grp, scale, kv, side, *, kv_col0, batch, seq, n_mem, tm, sub):
    M, D = x.shape
    assert all(w & (w - 1) == 0 for w in POOL_WINDOWS) and seq % tm == 0 and tm % sub == 0
    C = POOL_GROUP
    wc = POOL_CA_HEADS_PER_STEP * CA_HEAD_DIM
    n_mix, n_ca = len(POOL_WINDOWS), E_CA // wc
    spt = seq // tm
    ca_col0 = n_mix * 2 * C
    assert ca_col0 % (2 * wc) == 0 and kv_col0 % (2 * wc) == 0
    mix_blk = lambda j: jnp.minimum(j, n_mix - 1)
    ca_blk = lambda j: jnp.clip(j - n_mix, 0, n_ca - 1)
    blocks = [((tm, D), F32), ((1, D), F32), ((D, 2 * C), BF16), ((D, 2 * wc), BF16),
              ((1, C, C), BF16), ((1, C), F32), ((n_mem, 2 * wc), BF16),
              ((tm, C), BF16), ((tm, wc), BF16)]
    scratch = [((tm, D), BF16), ((n_mix, POOL_HALO, C), F32)]
    n_steps = n_mix + n_ca
    side_in, side_out, side_shapes, side_blk, side_columns = _side_cast_plan(
        side, lambda b, s, j: (b * spt + s) * n_steps + j, batch * spt * n_steps)
    return pl.pallas_call(
        functools.partial(_pool_layer_kernel, sub=sub, n_mix_steps=n_mix, row_chunk=128,
                          side_columns=side_columns),
        grid=(batch, spt, n_steps),
        in_specs=[pl.BlockSpec((tm, D), lambda b, s, j: (b * spt + s, 0)),
                  pl.BlockSpec((1, D), lambda b, s, j: (0, 0)),
                  pl.BlockSpec((D, 2 * C), lambda b, s, j: (0, mix_blk(j))),
                  pl.BlockSpec((D, 2 * wc), lambda b, s, j: (0, ca_col0 // (2 * wc) + ca_blk(j))),
                  pl.BlockSpec((1, C, C), lambda b, s, j: (mix_blk(j), 0, 0)),
                  pl.BlockSpec((1, C), lambda b, s, j: (0, mix_blk(j))),
                  pl.BlockSpec((n_mem, 2 * wc),
                               lambda b, s, j: (b, kv_col0 // (2 * wc) + ca_blk(j)))] + side_in,
        out_specs=[pl.BlockSpec((tm, C), lambda b, s, j: (b * spt + s, mix_blk(j))),
                   pl.BlockSpec((tm, wc), lambda b, s, j: (b * spt + s, ca_blk(j)))] + side_out,
        out_shape=[jax.ShapeDtypeStruct((M, E_MIX), BF16),
                   jax.ShapeDtypeStruct((M, E_CA), BF16)] + side_shapes,
        scratch_shapes=[pltpu.VMEM(s_, d_) for s_, d_ in scratch],
        compiler_params=_compiler_params(("arbitrary", "arbitrary", "arbitrary"),
                                         blocks + side_blk, scratch,
                                         temps=12 * _nbytes((sub + POOL_HALO, C), F32)),
        name="pool_layer",
    )(x, g.reshape(1, D), w_in, w_in, w_grp.astype(BF16), scale.reshape(1, E_MIX), kv,
      *[a for a, _, _, _ in side])


def _cumsum_rows(x):
    rows, width = x.shape
    r = lax.broadcasted_iota(jnp.int32, (F32_SUBLANES, width), 0)
    k = 1
    while k < F32_SUBLANES:
        sh = pltpu.roll(x, k, axis=0)
        top = jnp.where(r >= k, sh[:F32_SUBLANES], 0.0)
        x = x + jnp.concatenate([top, sh[F32_SUBLANES:]], axis=0)
        k *= 2
    while k < rows:
        x = jnp.concatenate([x[:k], x[k:] + x[:rows - k]], axis=0)
        k *= 2
    return x


def _hgrn_rows(qx, fx, vx, lb, states, causal, between):
    rows = qx.shape[0]
    assert rows == 2 * HG_SPAN
    qs = _silu(qx) * (HG_HEAD_DIM ** -0.5)
    fg = lb + (1.0 - lb) * jax.nn.sigmoid(fx)
    kk = 1.0 - fg
    lf = jnp.log(fg)
    between(0)
    sp, ch = HG_SPAN, HG_CHUNK
    g = [_cumsum_rows(lf[:sp]), _cumsum_rows(lf[sp:])]
    cat = lambda parts: jnp.concatenate(parts, axis=0)
    e_in = jnp.exp(cat(g))
    e_out = jnp.exp(cat([gp[sp - 1:] - gp for gp in g]))
    d_q = jnp.exp(cat([gp - gp[ch - 1:ch] for gp in g]))
    d_k = jnp.exp(cat([gp[ch - 1:ch] - gp for gp in g]))
    dec0, dec1 = e_in[sp - 1:sp], e_in[rows - 1:rows]
    q_in, k_out = qs * e_in, kk * e_out
    q_dg, k_dg = (qs * d_q).astype(BF16), (kk * d_k).astype(BF16)
    q_off, k_off = q_in[sp:].astype(BF16), k_out[:sp].astype(BF16)
    q_st = cat([q_in[:sp], q_in[sp:] * dec0]).astype(BF16)
    k_st = cat([k_out[:sp] * dec1, k_out[sp:]]).astype(BF16)
    dec = dec0 * dec1
    between(1)
    heads = range(len(states))
    lanes = [slice(h * HG_HEAD_DIM, (h + 1) * HG_HEAD_DIM) for h in heads]
    attn = []
    for h in heads:
        sl = lanes[h]
        a_dg = lax.dot_general(q_dg[:, sl], k_dg[:, sl], NT_DIMS, preferred_element_type=F32)
        a_dg = jnp.where(causal, a_dg, 0.0)
        a_off = lax.dot_general(q_off[:, sl], k_off[:, sl], NT_DIMS, preferred_element_type=F32)
        attn.append(cat([a_dg[:sp], jnp.concatenate([a_off, a_dg[sp:, sp:]], axis=1)]).astype(BF16))
    between(2)
    outs, new_states = [], []
    for h in heads:
        sl, st = lanes[h], states[h]
        o_h = (jnp.dot(attn[h], vx[:, sl], preferred_element_type=F32)
               + lax.dot_general(q_st[:, sl], st.astype(BF16), NT_DIMS, preferred_element_type=F32))
        new_states.append(st * dec[:, sl] + lax.dot_general(vx[:, sl], k_st[:, sl], TN_DIMS,
                                                            preferred_element_type=F32))
        ms = jnp.mean(o_h * o_h, axis=-1, keepdims=True)
        outs.append(o_h * lax.rsqrt(ms + EPS))
    return jnp.concatenate(outs, axis=1), new_states


def _hgrn_layer_kernel(h_ref, w_ref, gates_ref, kv_ref, *rest, layer, sub, n_mix_steps,
                       side_columns):
    n_side = len(side_columns)
    side_src, (bm_ref, bc_ref), rest = rest[:n_side], rest[n_side:n_side + 2], rest[n_side + 2:]
    side_dst, (st_ref,) = rest[:n_side], rest[n_side:]
    s = pl.program_id(1)
    j = pl.program_id(2)
    tm = h_ref.shape[0]
    n_sub = tm // sub
    W = bm_ref.shape[1]
    heads = W // HG_HEAD_DIM

    def project(r, w):
        return jnp.dot(h_ref[r * sub:(r + 1) * sub, :], w, preferred_element_type=F32)

    @pl.when((j == 0) & (s == 0))
    def _():
        st_ref[...] = jnp.zeros_like(st_ref)

    @pl.when(j < n_mix_steps)
    def _():
        _side_cast(side_src, side_dst, side_columns)
        sm = jax.nn.softmax(gates_ref[:DEPTH, :], axis=0)
        lb = jnp.sum(sm[:layer + 1], axis=0, keepdims=True) - sm[0:1]
        norm_gain = gates_ref[DEPTH:, :]
        r_i = lax.broadcasted_iota(jnp.int32, (sub, sub), 0)
        c_i = lax.broadcasted_iota(jnp.int32, (sub, sub), 1)
        causal = (r_i // HG_SPAN == c_i // HG_SPAN) & (c_i <= r_i)
        states = [st_ref[j * heads + h] for h in range(heads)]
        n_part = 4
        part = lambda k: w_ref[:, k * W:(k + 1) * W]
        cur = [project(0, part(k)) for k in range(n_part)]
        for r in range(n_sub):
            nxt = [None] * n_part

            def between(k, r=r, nxt=nxt):
                if r + 1 < n_sub:
                    nxt[k] = project(r + 1, part(k))

            qx, fx, ix, gx = cur
            o, states = _hgrn_rows(qx, fx, ix.astype(BF16), lb, states, causal, between)
            between(3)
            bm_ref[r * sub:(r + 1) * sub, :] = ((o * norm_gain) * _silu(gx)).astype(BF16)
            cur = nxt
        for h in range(heads):
            st_ref[j * heads + h] = states[h]

    @pl.when(j >= n_mix_steps)
    def _():
        _side_cast(side_src, side_dst, side_columns)
        _attention_step(project, w_ref, kv_ref, bc_ref, n_sub, sub)


HG_HEADS_PER_STEP = 2
HG_CA_HEADS_PER_STEP = 2


def _hgrn_weight_columns():
    W = HG_HEADS_PER_STEP * HG_HEAD_DIM
    wc = HG_CA_HEADS_PER_STEP * CA_HEAD_DIM
    gate0 = 3 * E_MIX + E_CA
    return _step_major_columns((0, E_MIX, 2 * E_MIX, gate0), W, E_MIX // W,
                               (3 * E_MIX, gate0 + E_MIX), wc, E_CA // wc)


def _hgrn_layer(h, w_in, gates, kv, side, *, kv_col0, layer, batch, seq, n_mem, tm, sub):
    M, D = h.shape
    W = HG_HEADS_PER_STEP * HG_HEAD_DIM
    wc = HG_CA_HEADS_PER_STEP * CA_HEAD_DIM
    tn = 4 * W
    assert seq % tm == 0 and tm % sub == 0 and sub == 2 * HG_SPAN
    assert tn == 2 * wc and kv_col0 % (2 * wc) == 0
    n_mix, n_ca = E_MIX // W, E_CA // wc
    spt = seq // tm
    mix_blk = lambda j: jnp.minimum(j, n_mix - 1)
    ca_blk = lambda j: jnp.clip(j - n_mix, 0, n_ca - 1)
    blocks = [((tm, D), BF16), ((D, tn), BF16), ((DEPTH + 1, W), F32), ((n_mem, 2 * wc), BF16),
              ((tm, W), BF16), ((tm, wc), BF16)]
    scratch = [((E_MIX // HG_HEAD_DIM, HG_HEAD_DIM, HG_HEAD_DIM), F32)]
    n_steps = n_mix + n_ca
    side_in, side_out, side_shapes, side_blk, side_columns = _side_cast_plan(
        side, lambda b, s, j: (b * spt + s) * n_steps + j, batch * spt * n_steps)
    return pl.pallas_call(
        functools.partial(_hgrn_layer_kernel, layer=layer, sub=sub, n_mix_steps=n_mix,
                          side_columns=side_columns),
        grid=(batch, spt, n_steps),
        in_specs=[pl.BlockSpec((tm, D), lambda b, s, j: (b * spt + s, 0)),
                  pl.BlockSpec((D, tn), lambda b, s, j: (0, j)),
                  pl.BlockSpec((DEPTH + 1, W), lambda b, s, j: (0, mix_blk(j))),
                  pl.BlockSpec((n_mem, 2 * wc),
                               lambda b, s, j: (b, kv_col0 // (2 * wc) + ca_blk(j)))] + side_in,
        out_specs=[pl.BlockSpec((tm, W), lambda b, s, j: (b * spt + s, mix_blk(j))),
                   pl.BlockSpec((tm, wc), lambda b, s, j: (b * spt + s, ca_blk(j)))] + side_out,
        out_shape=[jax.ShapeDtypeStruct((M, E_MIX), BF16),
                   jax.ShapeDtypeStruct((M, E_CA), BF16)] + side_shapes,
        scratch_shapes=[pltpu.VMEM(s_, d_) for s_, d_ in scratch],
        compiler_params=_compiler_params(("arbitrary", "arbitrary", "arbitrary"),
                                         blocks + side_blk, scratch,
                                         temps=32 * _nbytes((sub, W), F32)),
        name="hgrn_layer",
    )(h, w_in, gates, kv, *[a for a, _, _, _ in side])


def _out_proj_kernel(x_ref, bm_ref, bc_ref, wm_ref, wc_ref, g_ref, *out_refs, last):
    acc = jnp.dot(bm_ref[...], wm_ref[...], preferred_element_type=F32)
    acc = acc + jnp.dot(bc_ref[...], wc_ref[...], preferred_element_type=F32)
    y = x_ref[...] + acc
    normed = _rms_rows(y, g_ref[...])
    if last:
        out_refs[0][...] = normed
    else:
        out_refs[0][...] = y
        out_refs[1][...] = normed.astype(BF16)


def _out_proj(x, bm, bc, w_out, g, *, tm, last):
    M, D = x.shape
    assert M % tm == 0
    resident = pl.Buffered(1)
    row_blk = lambda w: pl.BlockSpec((tm, w), lambda i: (i, 0))
    out_specs, out_shape = [row_blk(D)], [jax.ShapeDtypeStruct((M, D), F32)]
    blocks = [((tm, D), F32), ((tm, E_MIX), BF16), ((tm, E_CA), BF16), ((1, D), F32), ((tm, D), F32)]
    if not last:
        out_specs.append(row_blk(D))
        out_shape.append(jax.ShapeDtypeStruct((M, D), BF16))
        blocks.append(((tm, D), BF16))
    single = [((E_MIX, D), BF16), ((E_CA, D), BF16)]
    return pl.pallas_call(
        functools.partial(_out_proj_kernel, last=last),
        grid=(M // tm,),
        in_specs=[row_blk(D), row_blk(E_MIX), row_blk(E_CA),
                  pl.BlockSpec((E_MIX, D), lambda i: (0, 0), pipeline_mode=resident),
                  pl.BlockSpec((E_CA, D), lambda i: (E_MIX // E_CA, 0), pipeline_mode=resident),
                  pl.BlockSpec((1, D), lambda i: (0, 0))],
        out_specs=out_specs,
        out_shape=out_shape,
        compiler_params=_compiler_params(("parallel",), blocks, single,
                                         temps=2 * _nbytes((tm, D), F32)),
        name="out_proj_last" if last else "out_proj",
    )(x, bm, bc, w_out, w_out, g.reshape(1, D))


def kernel(x, mem, norm_g, mem_norm_g, w_kv, w_out, pool_w_in, pool_w_grp, pool_scale,
           hgrn_w_in, hgrn_lb, hgrn_norm_g, final_g):
    batch, seq, d_model = x.shape
    n_mem = mem.shape[1]
    x2 = x.reshape(batch * seq, d_model)
    mem2 = mem.reshape(batch * n_mem, d_model)
    assert DEPTH == 2 and N_MIXERS == 2
    assert POOL_CA_HEADS_PER_STEP == HG_CA_HEADS_PER_STEP
    kv, w_pool = _kv_proj(mem2, mem_norm_g, w_kv, [(pool_w_in, 0, 128, _pool_weight_columns())],
                          tm=512, tn=POOL_CA_HEADS_PER_STEP * CA_HEAD_DIM)
    common = dict(batch=batch, seq=seq, n_mem=n_mem, sub=256)
    bm, bc, w_hg, w_o0 = _pool_layer(
        x2, norm_g[0], w_pool, pool_w_grp[0], pool_scale[0], kv,
        [(hgrn_w_in, 0, 32, _hgrn_weight_columns()), (w_out, 0, 64, None)],
        kv_col0=0, tm=1024, **common)
    x2, h = _out_proj(x2, bm, bc, w_o0, norm_g[1], tm=512, last=False)
    gates = jnp.concatenate([hgrn_lb, hgrn_norm_g[:1]], axis=0)
    bm, bc, w_o1 = _hgrn_layer(h, w_hg, gates, kv, [(w_out, 1, 64, None)],
                               kv_col0=2 * E_CA, layer=1, tm=2048, **common)
    y, = _out_proj(x2, bm, bc, w_o1, final_g, tm=512, last=True)
    return y.reshape(batch, seq, d_model)
```

```python
import functools

import jax
import jax.numpy as jnp
from jax import lax
from jax.experimental import pallas as pl
from jax.experimental.pallas import tpu as pltpu

F32 = jnp.float32
BF16 = jnp.bfloat16

DEPTH = 2
N_MIXERS = 2
E_MIX = 3072
E_CA = 1024
E_BRANCH = E_MIX + E_CA
CA_HEADS = 4
CA_HEAD_DIM = E_CA // CA_HEADS
POOL_WINDOWS = (2, 4, 8, 16)
POOL_GROUP = E_MIX // len(POOL_WINDOWS)
POOL_HALO = max(POOL_WINDOWS)
HG_HEAD_DIM = 128
HG_CHUNK = 64
HG_SPAN = 2 * HG_CHUNK
EPS = 1e-6

V7X_VMEM_BYTES = 64 * 1024 * 1024
VMEM_RESERVED_BYTES = 4 * 1024 * 1024
VMEM_MARGIN_DIV, VMEM_MARGIN_BYTES = 4, 2 * 1024 * 1024
F32_SUBLANES = 8

KV_PROJ_ROWS = 512
POOL_LAYER_ROWS = 1024
HGRN_LAYER_ROWS = 2048
OUT_PROJ_ROWS = 512
SUB_ROWS = 2 * HG_SPAN
NORM_ROW_CHUNK = 128
CA_HEADS_PER_STEP = 2
HG_HEADS_PER_STEP = 2
POOL_W_CAST_ROWS, HGRN_W_CAST_ROWS, W_OUT_CAST_ROWS = 128, 32, 64
NT_DIMS = (((1,), (1,)), ((), ()))
TN_DIMS = (((0,), (0,)), ((), ()))


def _nbytes(shape, dtype):
    n = 1
    for s in shape:
        n *= s
    return n * jnp.dtype(dtype).itemsize


def _compiler_params(semantics, blocks, scratch=(), temps=0):
    need = 2 * sum(_nbytes(s, d) for s, d in blocks) + sum(_nbytes(s, d) for s, d in scratch) + temps
    limit = min(need + need // VMEM_MARGIN_DIV + VMEM_MARGIN_BYTES,
                V7X_VMEM_BYTES - VMEM_RESERVED_BYTES)
    return pltpu.CompilerParams(dimension_semantics=semantics, vmem_limit_bytes=int(limit))


def _silu(x):
    return x * jax.nn.sigmoid(x)


def _rms_rows(x, g):
    ms = jnp.mean(x * x, axis=-1, keepdims=True)
    return (x * lax.rsqrt(ms + EPS)) * g


def _norm_rows_to(h_ref, x_ref, g_ref, row_chunk):
    def body(r, carry):
        rows = pl.ds(pl.multiple_of(r * row_chunk, row_chunk), row_chunk)
        h_ref[rows, :] = _rms_rows(x_ref[rows, :], g_ref[...]).astype(BF16)
        return carry
    lax.fori_loop(0, x_ref.shape[0] // row_chunk, body, 0)


def _attend(q, k, v):
    s = lax.dot_general(q, k, NT_DIMS, preferred_element_type=F32) * (CA_HEAD_DIM ** -0.5)
    e = jnp.exp(s - jnp.max(s, axis=-1, keepdims=True))
    p = e / jnp.sum(e, axis=-1, keepdims=True)
    return jnp.dot(p.astype(BF16), v, preferred_element_type=F32)


def _step_major_columns(mix_parts, mix_width, n_mix, ca_parts, ca_width, n_ca):
    order = []
    for j in range(n_mix):
        order += [(col0 + j * mix_width, mix_width) for col0 in mix_parts]
    for c in range(n_ca):
        order += [(col0 + c * ca_width, ca_width) for col0 in ca_parts]
    return tuple(order)


def _side_cast_plan(srcs, step_of, n_steps):
    in_specs, out_specs, out_shapes, blocks = [], [], [], []
    for arr, lead, rows, columns in srcs:
        R, N = arr.shape[-2:]
        nb = R // rows
        assert R % rows == 0 and nb <= n_steps
        assert columns is None or sorted(columns) == sorted(set(columns)) and \
            sum(w for _, w in columns) == N
        blk = lambda *ids, nb=nb: jnp.minimum(step_of(*ids), nb - 1)
        if lead is None:
            in_specs.append(pl.BlockSpec((rows, N), lambda *ids, blk=blk: (blk(*ids), 0)))
        else:
            in_specs.append(pl.BlockSpec((None, rows, N),
                                         lambda *ids, blk=blk, lead=lead: (lead, blk(*ids), 0)))
        out_specs.append(pl.BlockSpec((rows, N), lambda *ids, blk=blk: (blk(*ids), 0)))
        out_shapes.append(jax.ShapeDtypeStruct((R, N), BF16))
        blocks += [((rows, N), arr.dtype), ((rows, N), BF16)]
    return in_specs, out_specs, out_shapes, blocks, tuple(c for _, _, _, c in srcs)


def _side_cast(src_refs, dst_refs, columns):
    for src, dst, cols in zip(src_refs, dst_refs, columns):
        if cols is None:
            dst[...] = src[...].astype(BF16)
        else:
            dst[...] = jnp.concatenate([src[:, a:a + w] for a, w in cols], axis=1).astype(BF16)


def _kv_proj_kernel(x_ref, g_ref, w_ref, *rest, row_chunk, side_columns):
    n_side = len(side_columns)
    side_src, o_ref, side_dst, h_ref = (rest[:n_side], rest[n_side], rest[n_side + 1:-1],
                                        rest[-1])

    @pl.when((pl.program_id(1) == 0) & (pl.program_id(2) == 0))
    def _():
        _norm_rows_to(h_ref, x_ref, g_ref, row_chunk)

    _side_cast(side_src, side_dst, side_columns)
    acc = jnp.dot(h_ref[...], w_ref[...].astype(BF16), preferred_element_type=F32)
    o_ref[...] = acc.astype(o_ref.dtype)


def _kv_proj(mem, g, w_kv, side, *, tm, tn):
    M, D = mem.shape
    L, _, N = w_kv.shape
    assert M % tm == 0 and N == 2 * E_CA and E_CA % tn == 0
    blocks = [((tm, D), F32), ((1, D), F32), ((D, tn), w_kv.dtype), ((tm, tn), BF16)]
    grid = (M // tm, L, N // tn)
    n_pair = E_CA // tn
    out_blk = lambda j: jnp.where(j < n_pair, 2 * j, 2 * (j - n_pair) + 1)
    side_in, side_out, side_shapes, side_blk, side_columns = _side_cast_plan(
        side, lambda i, l, j: (i * grid[1] + l) * grid[2] + j, grid[0] * grid[1] * grid[2])
    return pl.pallas_call(
        functools.partial(_kv_proj_kernel, row_chunk=NORM_ROW_CHUNK, side_columns=side_columns),
        grid=grid,
        in_specs=[pl.BlockSpec((tm, D), lambda i, l, j: (i, 0)),
                  pl.BlockSpec((1, D), lambda i, l, j: (0, 0)),
                  pl.BlockSpec((None, D, tn), lambda i, l, j: (l, 0, j))] + side_in,
        out_specs=[pl.BlockSpec((tm, tn),
                                lambda i, l, j: (i, l * (N // tn) + out_blk(j)))] + side_out,
        out_shape=[jax.ShapeDtypeStruct((M, L * N), BF16)] + side_shapes,
        scratch_shapes=[pltpu.VMEM((tm, D), BF16)],
        compiler_params=_compiler_params(("arbitrary", "arbitrary", "arbitrary"),
                                         blocks + side_blk, [((tm, D), BF16)],
                                         temps=2 * _nbytes((tm, tn), F32)),
        name="kv_proj",
    )(mem, g.reshape(1, D), w_kv, *[a for a, _, _, _ in side])


def _attention_step(project, w_ref, kv_ref, bc_ref, n_sub, sub):
    wc = bc_ref.shape[1]
    for r in range(n_sub):
        q = project(r, w_ref[:, :wc]).astype(BF16)
        gate = project(r, w_ref[:, wc:])
        for hh in range(wc // CA_HEAD_DIM):
            sl = slice(hh * CA_HEAD_DIM, (hh + 1) * CA_HEAD_DIM)
            o = _attend(q[:, sl], kv_ref[:, sl],
                        kv_ref[:, wc + hh * CA_HEAD_DIM:wc + (hh + 1) * CA_HEAD_DIM])
            bc_ref[r * sub:(r + 1) * sub, sl] = (o * _silu(gate[:, sl])).astype(BF16)


def _pool_rows(u, halo, window, pos0):
    s = jnp.concatenate([halo, u], axis=0)
    k = 1
    while k < POOL_HALO:
        s = jnp.where(k < window, s + pltpu.roll(s, k, axis=0), s)
        k *= 2
    pos = pos0 + lax.broadcasted_iota(jnp.int32, u.shape, 0)
    cnt = jnp.minimum(pos + 1, window).astype(F32)
    return s[POOL_HALO:, :] / cnt - u


def _pool_layer_kernel(x_ref, g_ref, wmix_ref, wca_ref, wgrp_ref, scale_ref, kv_ref, *rest,
                       sub, n_mix_steps, row_chunk, side_columns):
    n_side = len(side_columns)
    side_src, (bm_ref, bc_ref), rest = rest[:n_side], rest[n_side:n_side + 2], rest[n_side + 2:]
    side_dst, (h_ref, halo_ref) = rest[:n_side], rest[n_side:]
    s = pl.program_id(1)
    j = pl.program_id(2)
    tm = x_ref.shape[0]
    n_sub = tm // sub
    C = bm_ref.shape[1]

    def project(r, w):
        return jnp.dot(h_ref[r * sub:(r + 1) * sub, :], w, preferred_element_type=F32)

    @pl.when(j == 0)
    def _():
        _norm_rows_to(h_ref, x_ref, g_ref, row_chunk)

    @pl.when(j < n_mix_steps)
    def _():
        _side_cast(side_src, side_dst, side_columns)
        window = jnp.int32(POOL_WINDOWS[0])
        for g in range(1, len(POOL_WINDOWS)):
            window = jnp.where(j == g, POOL_WINDOWS[g], window)
        halo = jnp.where(s == 0, 0.0, halo_ref[j])
        u, gate = project(0, wmix_ref[:, :C]), project(0, wmix_ref[:, C:])
        for r in range(n_sub):
            pooled = _pool_rows(u, halo, window, s * tm + r * sub)
            halo = u[sub - POOL_HALO:, :]
            u_next = project(r + 1, wmix_ref[:, :C]) if r + 1 < n_sub else None
            y = jnp.dot(pooled.astype(BF16), wgrp_ref[0], preferred_element_type=F32)
            gate_next = project(r + 1, wmix_ref[:, C:]) if r + 1 < n_sub else None
            bm_ref[r * sub:(r + 1) * sub, :] = ((y * scale_ref[...]) * _silu(gate)).astype(BF16)
            u, gate = u_next, gate_next
        halo_ref[j] = halo

    @pl.when(j >= n_mix_steps)
    def _():
        _side_cast(side_src, side_dst, side_columns)
        _attention_step(project, wca_ref, kv_ref, bc_ref, n_sub, sub)


def _pool_weight_columns():
    wc = CA_HEADS_PER_STEP * CA_HEAD_DIM
    gate0 = E_MIX + E_CA
    return _step_major_columns((0, gate0), POOL_GROUP, len(POOL_WINDOWS),
                               (E_MIX, gate0 + E_MIX), wc, E_CA // wc)


def _pool_layer(x, g, w_in, w_grp, scale, kv, side, *, kv_col0, batch, seq, n_mem, tm, sub):
    M, D = x.shape
    assert all(w & (w - 1) == 0 for w in POOL_WINDOWS) and seq % tm == 0 and tm % sub == 0
    C = POOL_GROUP
    wc = CA_HEADS_PER_STEP * CA_HEAD_DIM
    n_mix, n_ca = len(POOL_WINDOWS), E_CA // wc
    spt = seq // tm
    ca_col0 = n_mix * 2 * C
    assert ca_col0 % (2 * wc) == 0 and kv_col0 % (2 * wc) == 0
    mix_blk = lambda j: jnp.minimum(j, n_mix - 1)
    ca_blk = lambda j: jnp.clip(j - n_mix, 0, n_ca - 1)
    blocks = [((tm, D), F32), ((1, D), F32), ((D, 2 * C), BF16), ((D, 2 * wc), BF16),
              ((1, C, C), BF16), ((1, C), F32), ((n_mem, 2 * wc), BF16),
              ((tm, C), BF16), ((tm, wc), BF16)]
    scratch = [((tm, D), BF16), ((n_mix, POOL_HALO, C), F32)]
    n_steps = n_mix + n_ca
    side_in, side_out, side_shapes, side_blk, side_columns = _side_cast_plan(
        side, lambda b, s, j: (b * spt + s) * n_steps + j, batch * spt * n_steps)
    return pl.pallas_call(
        functools.partial(_pool_layer_kernel, sub=sub, n_mix_steps=n_mix,
                          row_chunk=NORM_ROW_CHUNK,
                          side_columns=side_columns),
        grid=(batch, spt, n_steps),
        in_specs=[pl.BlockSpec((tm, D), lambda b, s, j: (b * spt + s, 0)),
                  pl.BlockSpec((1, D), lambda b, s, j: (0, 0)),
                  pl.BlockSpec((D, 2 * C), lambda b, s, j: (0, mix_blk(j))),
                  pl.BlockSpec((D, 2 * wc), lambda b, s, j: (0, ca_col0 // (2 * wc) + ca_blk(j))),
                  pl.BlockSpec((1, C, C), lambda b, s, j: (mix_blk(j), 0, 0)),
                  pl.BlockSpec((1, C), lambda b, s, j: (0, mix_blk(j))),
                  pl.BlockSpec((n_mem, 2 * wc),
                               lambda b, s, j: (b, kv_col0 // (2 * wc) + ca_blk(j)))] + side_in,
        out_specs=[pl.BlockSpec((tm, C), lambda b, s, j: (b * spt + s, mix_blk(j))),
                   pl.BlockSpec((tm, wc), lambda b, s, j: (b * spt + s, ca_blk(j)))] + side_out,
        out_shape=[jax.ShapeDtypeStruct((M, E_MIX), BF16),
                   jax.ShapeDtypeStruct((M, E_CA), BF16)] + side_shapes,
        scratch_shapes=[pltpu.VMEM(s_, d_) for s_, d_ in scratch],
        compiler_params=_compiler_params(("arbitrary", "arbitrary", "arbitrary"),
                                         blocks + side_blk, scratch,
                                         temps=12 * _nbytes((sub + POOL_HALO, C), F32)),
        name="pool_layer",
    )(x, g.reshape(1, D), w_in, w_in, w_grp.astype(BF16), scale.reshape(1, E_MIX), kv,
      *[a for a, _, _, _ in side])


def _cumsum_rows(x):
    rows, width = x.shape
    r = lax.broadcasted_iota(jnp.int32, (F32_SUBLANES, width), 0)
    k = 1
    while k < F32_SUBLANES:
        sh = pltpu.roll(x, k, axis=0)
        top = jnp.where(r >= k, sh[:F32_SUBLANES], 0.0)
        x = x + jnp.concatenate([top, sh[F32_SUBLANES:]], axis=0)
        k *= 2
    while k < rows:
        x = jnp.concatenate([x[:k], x[k:] + x[:rows - k]], axis=0)
        k *= 2
    return x


def _hgrn_rows(qx, fx, vx, lb, states, causal, between):
    rows = qx.shape[0]
    assert rows == 2 * HG_SPAN
    qs = _silu(qx) * (HG_HEAD_DIM ** -0.5)
    fg = lb + (1.0 - lb) * jax.nn.sigmoid(fx)
    kk = 1.0 - fg
    lf = jnp.log(fg)
    between(0)
    sp, ch = HG_SPAN, HG_CHUNK
    g = [_cumsum_rows(lf[:sp]), _cumsum_rows(lf[sp:])]
    cat = lambda parts: jnp.concatenate(parts, axis=0)
    e_in = jnp.exp(cat(g))
    e_out = jnp.exp(cat([gp[sp - 1:] - gp for gp in g]))
    d_q = jnp.exp(cat([gp - gp[ch - 1:ch] for gp in g]))
    d_k = jnp.exp(cat([gp[ch - 1:ch] - gp for gp in g]))
    dec0, dec1 = e_in[sp - 1:sp], e_in[rows - 1:rows]
    q_in, k_out = qs * e_in, kk * e_out
    q_dg, k_dg = (qs * d_q).astype(BF16), (kk * d_k).astype(BF16)
    q_off, k_off = q_in[sp:].astype(BF16), k_out[:sp].astype(BF16)
    q_st = cat([q_in[:sp], q_in[sp:] * dec0]).astype(BF16)
    k_st = cat([k_out[:sp] * dec1, k_out[sp:]]).astype(BF16)
    dec = dec0 * dec1
    between(1)
    heads = range(len(states))
    lanes = [slice(h * HG_HEAD_DIM, (h + 1) * HG_HEAD_DIM) for h in heads]
    attn = []
    for h in heads:
        sl = lanes[h]
        a_dg = lax.dot_general(q_dg[:, sl], k_dg[:, sl], NT_DIMS, preferred_element_type=F32)
        a_dg = jnp.where(causal, a_dg, 0.0)
        a_off = lax.dot_general(q_off[:, sl], k_off[:, sl], NT_DIMS, preferred_element_type=F32)
        attn.append(cat([a_dg[:sp], jnp.concatenate([a_off, a_dg[sp:, sp:]], axis=1)]).astype(BF16))
    between(2)
    outs, new_states = [], []
    for h in heads:
        sl, st = lanes[h], states[h]
        o_h = (jnp.dot(attn[h], vx[:, sl], preferred_element_type=F32)
               + lax.dot_general(q_st[:, sl], st.astype(BF16), NT_DIMS, preferred_element_type=F32))
        new_states.append(st * dec[:, sl] + lax.dot_general(vx[:, sl], k_st[:, sl], TN_DIMS,
                                                            preferred_element_type=F32))
        ms = jnp.mean(o_h * o_h, axis=-1, keepdims=True)
        outs.append(o_h * lax.rsqrt(ms + EPS))
    return jnp.concatenate(outs, axis=1), new_states


def _hgrn_layer_kernel(h_ref, w_ref, gates_ref, kv_ref, *rest, layer, sub, n_mix_steps,
                       side_columns):
    n_side = len(side_columns)
    side_src, (bm_ref, bc_ref), rest = rest[:n_side], rest[n_side:n_side + 2], rest[n_side + 2:]
    side_dst, (st_ref,) = rest[:n_side], rest[n_side:]
    s = pl.program_id(1)
    j = pl.program_id(2)
    tm = h_ref.shape[0]
    n_sub = tm // sub
    W = bm_ref.shape[1]
    heads = W // HG_HEAD_DIM

    def project(r, w):
        return jnp.dot(h_ref[r * sub:(r + 1) * sub, :], w, preferred_element_type=F32)

    @pl.when((j == 0) & (s == 0))
    def _():
        st_ref[...] = jnp.zeros_like(st_ref)

    @pl.when(j < n_mix_steps)
    def _():
        _side_cast(side_src, side_dst, side_columns)
        sm = jax.nn.softmax(gates_ref[:DEPTH, :], axis=0)
        lb = jnp.sum(sm[:layer + 1], axis=0, keepdims=True) - sm[0:1]
        norm_gain = gates_ref[DEPTH:, :]
        r_i = lax.broadcasted_iota(jnp.int32, (sub, sub), 0)
        c_i = lax.broadcasted_iota(jnp.int32, (sub, sub), 1)
        causal = (r_i // HG_SPAN == c_i // HG_SPAN) & (c_i <= r_i)
        states = [st_ref[j * heads + h] for h in range(heads)]
        n_part = 4
        part = lambda k: w_ref[:, k * W:(k + 1) * W]
        cur = [project(0, part(k)) for k in range(n_part)]
        for r in range(n_sub):
            nxt = [None] * n_part

            def between(k, r=r, nxt=nxt):
                if r + 1 < n_sub:
                    nxt[k] = project(r + 1, part(k))

            qx, fx, ix, gx = cur
            o, states = _hgrn_rows(qx, fx, ix.astype(BF16), lb, states, causal, between)
            between(3)
            bm_ref[r * sub:(r + 1) * sub, :] = ((o * norm_gain) * _silu(gx)).astype(BF16)
            cur = nxt
        for h in range(heads):
            st_ref[j * heads + h] = states[h]

    @pl.when(j >= n_mix_steps)
    def _():
        _side_cast(side_src, side_dst, side_columns)
        _attention_step(project, w_ref, kv_ref, bc_ref, n_sub, sub)


def _hgrn_weight_columns():
    W = HG_HEADS_PER_STEP * HG_HEAD_DIM
    wc = CA_HEADS_PER_STEP * CA_HEAD_DIM
    gate0 = 3 * E_MIX + E_CA
    return _step_major_columns((0, E_MIX, 2 * E_MIX, gate0), W, E_MIX // W,
                               (3 * E_MIX, gate0 + E_MIX), wc, E_CA // wc)


def _hgrn_layer(h, w_in, gates, kv, side, *, kv_col0, layer, batch, seq, n_mem, tm, sub):
    M, D = h.shape
    W = HG_HEADS_PER_STEP * HG_HEAD_DIM
    wc = CA_HEADS_PER_STEP * CA_HEAD_DIM
    tn = 4 * W
    assert seq % tm == 0 and tm % sub == 0 and sub == 2 * HG_SPAN
    assert tn == 2 * wc and kv_col0 % (2 * wc) == 0
    n_mix, n_ca = E_MIX // W, E_CA // wc
    spt = seq // tm
    mix_blk = lambda j: jnp.minimum(j, n_mix - 1)
    ca_blk = lambda j: jnp.clip(j - n_mix, 0, n_ca - 1)
    blocks = [((tm, D), BF16), ((D, tn), BF16), ((DEPTH + 1, W), F32), ((n_mem, 2 * wc), BF16),
              ((tm, W), BF16), ((tm, wc), BF16)]
    scratch = [((E_MIX // HG_HEAD_DIM, HG_HEAD_DIM, HG_HEAD_DIM), F32)]
    n_steps = n_mix + n_ca
    side_in, side_out, side_shapes, side_blk, side_columns = _side_cast_plan(
        side, lambda b, s, j: (b * spt + s) * n_steps + j, batch * spt * n_steps)
    return pl.pallas_call(
        functools.partial(_hgrn_layer_kernel, layer=layer, sub=sub, n_mix_steps=n_mix,
                          side_columns=side_columns),
        grid=(batch, spt, n_steps),
        in_specs=[pl.BlockSpec((tm, D), lambda b, s, j: (b * spt + s, 0)),
                  pl.BlockSpec((D, tn), lambda b, s, j: (0, j)),
                  pl.BlockSpec((DEPTH + 1, W), lambda b, s, j: (0, mix_blk(j))),
                  pl.BlockSpec((n_mem, 2 * wc),
                               lambda b, s, j: (b, kv_col0 // (2 * wc) + ca_blk(j)))] + side_in,
        out_specs=[pl.BlockSpec((tm, W), lambda b, s, j: (b * spt + s, mix_blk(j))),
                   pl.BlockSpec((tm, wc), lambda b, s, j: (b * spt + s, ca_blk(j)))] + side_out,
        out_shape=[jax.ShapeDtypeStruct((M, E_MIX), BF16),
                   jax.ShapeDtypeStruct((M, E_CA), BF16)] + side_shapes,
        scratch_shapes=[pltpu.VMEM(s_, d_) for s_, d_ in scratch],
        compiler_params=_compiler_params(("arbitrary", "arbitrary", "arbitrary"),
                                         blocks + side_blk, scratch,
                                         temps=32 * _nbytes((sub, W), F32)),
        name="hgrn_layer",
    )(h, w_in, gates, kv, *[a for a, _, _, _ in side])


def _out_proj_kernel(x_ref, bm_ref, bc_ref, wm_ref, wc_ref, g_ref, *out_refs, last):
    acc = jnp.dot(bm_ref[...], wm_ref[...], preferred_element_type=F32)
    acc = acc + jnp.dot(bc_ref[...], wc_ref[...], preferred_element_type=F32)
    y = x_ref[...] + acc
    normed = _rms_rows(y, g_ref[...])
    if last:
        out_refs[0][...] = normed
    else:
        out_refs[0][...] = y
        out_refs[1][...] = normed.astype(BF16)


def _out_proj(x, bm, bc, w_out, g, *, tm, last):
    M, D = x.shape
    assert M % tm == 0
    resident = pl.Buffered(1)
    row_blk = lambda w: pl.BlockSpec((tm, w), lambda i: (i, 0))
    out_specs, out_shape = [row_blk(D)], [jax.ShapeDtypeStruct((M, D), F32)]
    blocks = [((tm, D), F32), ((tm, E_MIX), BF16), ((tm, E_CA), BF16), ((1, D), F32), ((tm, D), F32)]
    if not last:
        out_specs.append(row_blk(D))
        out_shape.append(jax.ShapeDtypeStruct((M, D), BF16))
        blocks.append(((tm, D), BF16))
    single = [((E_MIX, D), BF16), ((E_CA, D), BF16)]
    return pl.pallas_call(
        functools.partial(_out_proj_kernel, last=last),
        grid=(M // tm,),
        in_specs=[row_blk(D), row_blk(E_MIX), row_blk(E_CA),
                  pl.BlockSpec((E_MIX, D), lambda i: (0, 0), pipeline_mode=resident),
                  pl.BlockSpec((E_CA, D), lambda i: (E_MIX // E_CA, 0), pipeline_mode=resident),
                  pl.BlockSpec((1, D), lambda i: (0, 0))],
        out_specs=out_specs,
        out_shape=out_shape,
        compiler_params=_compiler_params(("parallel",), blocks, single,
                                         temps=2 * _nbytes((tm, D), F32)),
        name="out_proj_last" if last else "out_proj",
    )(x, bm, bc, w_out, w_out, g.reshape(1, D))


def kernel(x, mem, norm_g, mem_norm_g, w_kv, w_out, pool_w_in, pool_w_grp, pool_scale,
           hgrn_w_in, hgrn_lb, hgrn_norm_g, final_g):
    batch, seq, d_model = x.shape
    n_mem = mem.shape[1]
    x2 = x.reshape(batch * seq, d_model)
    mem2 = mem.reshape(batch * n_mem, d_model)
    assert DEPTH == 2 and N_MIXERS == 2
    kv, w_pool = _kv_proj(
        mem2, mem_norm_g, w_kv, [(pool_w_in, 0, POOL_W_CAST_ROWS, _pool_weight_columns())],
        tm=KV_PROJ_ROWS, tn=CA_HEADS_PER_STEP * CA_HEAD_DIM)
    common = dict(batch=batch, seq=seq, n_mem=n_mem, sub=SUB_ROWS)
    bm, bc, w_hg, w_o0 = _pool_layer(
        x2, norm_g[0], w_pool, pool_w_grp[0], pool_scale[0], kv,
        [(hgrn_w_in, 0, HGRN_W_CAST_ROWS, _hgrn_weight_columns()),
         (w_out, 0, W_OUT_CAST_ROWS, None)],
        kv_col0=0, tm=POOL_LAYER_ROWS, **common)
    x2, h = _out_proj(x2, bm, bc, w_o0, norm_g[1], tm=OUT_PROJ_ROWS, last=False)
    gates = jnp.concatenate([hgrn_lb, hgrn_norm_g[:1]], axis=0)
    bm, bc, w_o1 = _hgrn_layer(h, w_hg, gates, kv, [(w_out, 1, W_OUT_CAST_ROWS, None)],
                               kv_col0=2 * E_CA, layer=1, tm=HGRN_LAYER_ROWS, **common)
    y, = _out_proj(x2, bm, bc, w_o1, final_g, tm=OUT_PROJ_ROWS, last=True)
    return y.reshape(batch, seq, d_model)
```

```python
import functools

import jax
import jax.numpy as jnp
from jax import lax
from jax.experimental import pallas as pl
from jax.experimental.pallas import tpu as pltpu

F32 = jnp.float32
BF16 = jnp.bfloat16

DEPTH = 2
N_MIXERS = 2
E_MIX = 3072
E_CA = 1024
E_BRANCH = E_MIX + E_CA
CA_HEADS = 4
CA_HEAD_DIM = E_CA // CA_HEADS
POOL_WINDOWS = (2, 4, 8, 16)
POOL_GROUP = E_MIX // len(POOL_WINDOWS)
POOL_HALO = max(POOL_WINDOWS)
HG_HEAD_DIM = 128
HG_CHUNK = 64
HG_SPAN = 2 * HG_CHUNK
EPS = 1e-6

V7X_VMEM_BYTES = 64 * 1024 * 1024
VMEM_RESERVED_BYTES = 4 * 1024 * 1024
VMEM_MARGIN_DIV, VMEM_MARGIN_BYTES = 4, 2 * 1024 * 1024
F32_SUBLANES = 8

KV_PROJ_ROWS = 512
POOL_LAYER_ROWS = 1024
HGRN_LAYER_ROWS = 2048
OUT_PROJ_ROWS = 512
SUB_ROWS = 2 * HG_SPAN
NORM_ROW_CHUNK = 128
CA_HEADS_PER_STEP = 2
HG_HEADS_PER_STEP = 2
POOL_W_CAST_ROWS, HGRN_W_CAST_ROWS, W_OUT_CAST_ROWS = 128, 32, 64
NT_DIMS = (((1,), (1,)), ((), ()))
TN_DIMS = (((0,), (0,)), ((), ()))


def _nbytes(shape, dtype):
    n = 1
    for s in shape:
        n *= s
    return n * jnp.dtype(dtype).itemsize


def _compiler_params(semantics, blocks, scratch=(), temps=0):
    need = 2 * sum(_nbytes(s, d) for s, d in blocks) + sum(_nbytes(s, d) for s, d in scratch) + temps
    limit = min(need + need // VMEM_MARGIN_DIV + VMEM_MARGIN_BYTES,
                V7X_VMEM_BYTES - VMEM_RESERVED_BYTES)
    return pltpu.CompilerParams(dimension_semantics=semantics, vmem_limit_bytes=int(limit))


def _silu(x):
    return x * jax.nn.sigmoid(x)


def _rms_rows(x, g):
    ms = jnp.mean(x * x, axis=-1, keepdims=True)
    return (x * lax.rsqrt(ms + EPS)) * g


def _norm_rows_to(h_ref, x_ref, g_ref, row_chunk):
    def body(r, carry):
        rows = pl.ds(pl.multiple_of(r * row_chunk, row_chunk), row_chunk)
        h_ref[rows, :] = _rms_rows(x_ref[rows, :], g_ref[...]).astype(BF16)
        return carry
    lax.fori_loop(0, x_ref.shape[0] // row_chunk, body, 0)


def _attend(q, k, v):
    s = lax.dot_general(q, k, NT_DIMS, preferred_element_type=F32) * (CA_HEAD_DIM ** -0.5)
    e = jnp.exp(s - jnp.max(s, axis=-1, keepdims=True))
    p = e / jnp.sum(e, axis=-1, keepdims=True)
    return jnp.dot(p.astype(BF16), v, preferred_element_type=F32)


def _step_major_columns(mix_parts, mix_width, n_mix, ca_parts, ca_width, n_ca):
    order = []
    for j in range(n_mix):
        order += [(col0 + j * mix_width, mix_width) for col0 in mix_parts]
    for c in range(n_ca):
        order += [(col0 + c * ca_width, ca_width) for col0 in ca_parts]
    return tuple(order)


def _side_cast_plan(srcs, step_of, n_steps):
    in_specs, out_specs, out_shapes, blocks = [], [], [], []
    for arr, lead, rows, columns in srcs:
        R, N = arr.shape[-2:]
        nb = R // rows
        assert R % rows == 0 and nb <= n_steps
        assert columns is None or sorted(columns) == sorted(set(columns)) and \
            sum(w for _, w in columns) == N
        blk = lambda *ids, nb=nb: jnp.minimum(step_of(*ids), nb - 1)
        if lead is None:
            in_specs.append(pl.BlockSpec((rows, N), lambda *ids, blk=blk: (blk(*ids), 0)))
        else:
            in_specs.append(pl.BlockSpec((None, rows, N),
                                         lambda *ids, blk=blk, lead=lead: (lead, blk(*ids), 0)))
        out_specs.append(pl.BlockSpec((rows, N), lambda *ids, blk=blk: (blk(*ids), 0)))
        out_shapes.append(jax.ShapeDtypeStruct((R, N), BF16))
        blocks += [((rows, N), arr.dtype), ((rows, N), BF16)]
    return in_specs, out_specs, out_shapes, blocks, tuple(c for _, _, _, c in srcs)


def _side_cast(src_refs, dst_refs, columns):
    for src, dst, cols in zip(src_refs, dst_refs, columns):
        if cols is None:
            dst[...] = src[...].astype(BF16)
        else:
            dst[...] = jnp.concatenate([src[:, a:a + w] for a, w in cols], axis=1).astype(BF16)


def _kv_proj_kernel(x_ref, g_ref, w_ref, *rest, row_chunk, side_columns):
    n_side = len(side_columns)
    side_src, o_ref, side_dst, h_ref = (rest[:n_side], rest[n_side], rest[n_side + 1:-1],
                                        rest[-1])

    @pl.when((pl.program_id(1) == 0) & (pl.program_id(2) == 0))
    def _():
        _norm_rows_to(h_ref, x_ref, g_ref, row_chunk)

    _side_cast(side_src, side_dst, side_columns)
    acc = jnp.dot(h_ref[...], w_ref[...].astype(BF16), preferred_element_type=F32)
    o_ref[...] = acc.astype(o_ref.dtype)


def _kv_proj(mem, g, w_kv, side, *, tm, tn):
    M, D = mem.shape
    L, _, N = w_kv.shape
    assert M % tm == 0 and N == 2 * E_CA and E_CA % tn == 0
    blocks = [((tm, D), F32), ((1, D), F32), ((D, tn), w_kv.dtype), ((tm, tn), BF16)]
    grid = (M // tm, L, N // tn)
    n_pair = E_CA // tn
    out_blk = lambda j: jnp.where(j < n_pair, 2 * j, 2 * (j - n_pair) + 1)
    side_in, side_out, side_shapes, side_blk, side_columns = _side_cast_plan(
        side, lambda i, l, j: (i * grid[1] + l) * grid[2] + j, grid[0] * grid[1] * grid[2])
    return pl.pallas_call(
        functools.partial(_kv_proj_kernel, row_chunk=NORM_ROW_CHUNK, side_columns=side_columns),
        grid=grid,
        in_specs=[pl.BlockSpec((tm, D), lambda i, l, j: (i, 0)),
                  pl.BlockSpec((1, D), lambda i, l, j: (0, 0)),
                  pl.BlockSpec((None, D, tn), lambda i, l, j: (l, 0, j))] + side_in,
        out_specs=[pl.BlockSpec((tm, tn),
                                lambda i, l, j: (i, l * (N // tn) + out_blk(j)))] + side_out,
        out_shape=[jax.ShapeDtypeStruct((M, L * N), BF16)] + side_shapes,
        scratch_shapes=[pltpu.VMEM((tm, D), BF16)],
        compiler_params=_compiler_params(("arbitrary", "arbitrary", "arbitrary"),
                                         blocks + side_blk, [((tm, D), BF16)],
                                         temps=2 * _nbytes((tm, tn), F32)),
        name="kv_proj",
    )(mem, g.reshape(1, D), w_kv, *[a for a, _, _, _ in side])


def _attention_step(project, w_ref, kv_ref, bc_ref, n_sub, sub):
    wc = bc_ref.shape[1]
    for r in range(n_sub):
        q = project(r, w_ref[:, :wc]).astype(BF16)
        gate = project(r, w_ref[:, wc:])
        for hh in range(wc // CA_HEAD_DIM):
            sl = slice(hh * CA_HEAD_DIM, (hh + 1) * CA_HEAD_DIM)
            o = _attend(q[:, sl], kv_ref[:, sl],
                        kv_ref[:, wc + hh * CA_HEAD_DIM:wc + (hh + 1) * CA_HEAD_DIM])
            bc_ref[r * sub:(r + 1) * sub, sl] = (o * _silu(gate[:, sl])).astype(BF16)


def _pool_rows(u, halo, window, pos0):
    s = jnp.concatenate([halo, u], axis=0)
    k = 1
    while k < POOL_HALO:
        s = jnp.where(k < window, s + pltpu.roll(s, k, axis=0), s)
        k *= 2
    pos = pos0 + lax.broadcasted_iota(jnp.int32, u.shape, 0)
    cnt = jnp.minimum(pos + 1, window).astype(F32)
    return s[POOL_HALO:, :] / cnt - u


def _pool_layer_kernel(x_ref, g_ref, wmix_ref, wca_ref, wgrp_ref, scale_ref, kv_ref, *rest,
                       sub, n_ca_steps, row_chunk, side_columns):
    n_side = len(side_columns)
    side_src, (bm_ref, bc_ref), rest = rest[:n_side], rest[n_side:n_side + 2], rest[n_side + 2:]
    side_dst, (h_ref, halo_ref) = rest[:n_side], rest[n_side:]
    s = pl.program_id(1)
    j = pl.program_id(2)
    tm = x_ref.shape[0]
    n_sub = tm // sub
    C = bm_ref.shape[1]

    def project(r, w):
        return jnp.dot(h_ref[r * sub:(r + 1) * sub, :], w, preferred_element_type=F32)

    @pl.when(j == 0)
    def _():
        _norm_rows_to(h_ref, x_ref, g_ref, row_chunk)

    @pl.when(j >= n_ca_steps)
    def _():
        _side_cast(side_src, side_dst, side_columns)
        grp = j - n_ca_steps
        window = jnp.int32(POOL_WINDOWS[0])
        for g in range(1, len(POOL_WINDOWS)):
            window = jnp.where(grp == g, POOL_WINDOWS[g], window)
        halo = jnp.where(s == 0, 0.0, halo_ref[grp])
        u, gate = project(0, wmix_ref[:, :C]), project(0, wmix_ref[:, C:])
        for r in range(n_sub):
            pooled = _pool_rows(u, halo, window, s * tm + r * sub)
            halo = u[sub - POOL_HALO:, :]
            u_next = project(r + 1, wmix_ref[:, :C]) if r + 1 < n_sub else None
            y = jnp.dot(pooled.astype(BF16), wgrp_ref[0], preferred_element_type=F32)
            gate_next = project(r + 1, wmix_ref[:, C:]) if r + 1 < n_sub else None
            bm_ref[r * sub:(r + 1) * sub, :] = ((y * scale_ref[...]) * _silu(gate)).astype(BF16)
            u, gate = u_next, gate_next
        halo_ref[grp] = halo

    @pl.when(j < n_ca_steps)
    def _():
        _side_cast(side_src, side_dst, side_columns)
        _attention_step(project, wca_ref, kv_ref, bc_ref, n_sub, sub)


def _pool_weight_columns():
    wc = CA_HEADS_PER_STEP * CA_HEAD_DIM
    gate0 = E_MIX + E_CA
    return _step_major_columns((0, gate0), POOL_GROUP, len(POOL_WINDOWS),
                               (E_MIX, gate0 + E_MIX), wc, E_CA // wc)


def _pool_layer(x, g, w_in, w_grp, scale, kv, side, *, kv_col0, batch, seq, n_mem, tm, sub):
    M, D = x.shape
    assert all(w & (w - 1) == 0 for w in POOL_WINDOWS) and seq % tm == 0 and tm % sub == 0
    C = POOL_GROUP
    wc = CA_HEADS_PER_STEP * CA_HEAD_DIM
    n_mix, n_ca = len(POOL_WINDOWS), E_CA // wc
    spt = seq // tm
    ca_col0 = n_mix * 2 * C
    assert ca_col0 % (2 * wc) == 0 and kv_col0 % (2 * wc) == 0
    ca_blk = lambda j: jnp.minimum(j, n_ca - 1)
    mix_blk = lambda j: jnp.clip(j - n_ca, 0, n_mix - 1)
    blocks = [((tm, D), F32), ((1, D), F32), ((D, 2 * C), BF16), ((D, 2 * wc), BF16),
              ((1, C, C), BF16), ((1, C), F32), ((n_mem, 2 * wc), BF16),
              ((tm, C), BF16), ((tm, wc), BF16)]
    scratch = [((tm, D), BF16), ((n_mix, POOL_HALO, C), F32)]
    n_steps = n_mix + n_ca
    side_in, side_out, side_shapes, side_blk, side_columns = _side_cast_plan(
        side, lambda b, s, j: (b * spt + s) * n_steps + j, batch * spt * n_steps)
    return pl.pallas_call(
        functools.partial(_pool_layer_kernel, sub=sub, n_ca_steps=n_ca,
                          row_chunk=NORM_ROW_CHUNK,
                          side_columns=side_columns),
        grid=(batch, spt, n_steps),
        in_specs=[pl.BlockSpec((tm, D), lambda b, s, j: (b * spt + s, 0)),
                  pl.BlockSpec((1, D), lambda b, s, j: (0, 0)),
                  pl.BlockSpec((D, 2 * C), lambda b, s, j: (0, mix_blk(j))),
                  pl.BlockSpec((D, 2 * wc), lambda b, s, j: (0, ca_col0 // (2 * wc) + ca_blk(j))),
                  pl.BlockSpec((1, C, C), lambda b, s, j: (mix_blk(j), 0, 0)),
                  pl.BlockSpec((1, C), lambda b, s, j: (0, mix_blk(j))),
                  pl.BlockSpec((n_mem, 2 * wc),
                               lambda b, s, j: (b, kv_col0 // (2 * wc) + ca_blk(j)))] + side_in,
        out_specs=[pl.BlockSpec((tm, C), lambda b, s, j: (b * spt + s, mix_blk(j))),
                   pl.BlockSpec((tm, wc), lambda b, s, j: (b * spt + s, ca_blk(j)))] + side_out,
        out_shape=[jax.ShapeDtypeStruct((M, E_MIX), BF16),
                   jax.ShapeDtypeStruct((M, E_CA), BF16)] + side_shapes,
        scratch_shapes=[pltpu.VMEM(s_, d_) for s_, d_ in scratch],
        compiler_params=_compiler_params(("arbitrary", "arbitrary", "arbitrary"),
                                         blocks + side_blk, scratch,
                                         temps=12 * _nbytes((sub + POOL_HALO, C), F32)),
        name="pool_layer",
    )(x, g.reshape(1, D), w_in, w_in, w_grp.astype(BF16), scale.reshape(1, E_MIX), kv,
      *[a for a, _, _, _ in side])


def _cumsum_rows(x):
    rows, width = x.shape
    r = lax.broadcasted_iota(jnp.int32, (F32_SUBLANES, width), 0)
    k = 1
    while k < F32_SUBLANES:
        sh = pltpu.roll(x, k, axis=0)
        top = jnp.where(r >= k, sh[:F32_SUBLANES], 0.0)
        x = x + jnp.concatenate([top, sh[F32_SUBLANES:]], axis=0)
        k *= 2
    while k < rows:
        x = jnp.concatenate([x[:k], x[k:] + x[:rows - k]], axis=0)
        k *= 2
    return x


def _hgrn_rows(qx, fx, vx, lb, states, causal, between):
    rows = qx.shape[0]
    assert rows == 2 * HG_SPAN
    qs = _silu(qx) * (HG_HEAD_DIM ** -0.5)
    fg = lb + (1.0 - lb) * jax.nn.sigmoid(fx)
    kk = 1.0 - fg
    lf = jnp.log(fg)
    between(0)
    sp, ch = HG_SPAN, HG_CHUNK
    g = [_cumsum_rows(lf[:sp]), _cumsum_rows(lf[sp:])]
    cat = lambda parts: jnp.concatenate(parts, axis=0)
    e_in = jnp.exp(cat(g))
    e_out = jnp.exp(cat([gp[sp - 1:] - gp for gp in g]))
    d_q = jnp.exp(cat([gp - gp[ch - 1:ch] for gp in g]))
    d_k = jnp.exp(cat([gp[ch - 1:ch] - gp for gp in g]))
    dec0, dec1 = e_in[sp - 1:sp], e_in[rows - 1:rows]
    q_in, k_out = qs * e_in, kk * e_out
    q_dg, k_dg = (qs * d_q).astype(BF16), (kk * d_k).astype(BF16)
    q_off, k_off = q_in[sp:].astype(BF16), k_out[:sp].astype(BF16)
    q_st = cat([q_in[:sp], q_in[sp:] * dec0]).astype(BF16)
    k_st = cat([k_out[:sp] * dec1, k_out[sp:]]).astype(BF16)
    dec = dec0 * dec1
    between(1)
    heads = range(len(states))
    lanes = [slice(h * HG_HEAD_DIM, (h + 1) * HG_HEAD_DIM) for h in heads]
    attn = []
    for h in heads:
        sl = lanes[h]
        a_dg = lax.dot_general(q_dg[:, sl], k_dg[:, sl], NT_DIMS, preferred_element_type=F32)
        a_dg = jnp.where(causal, a_dg, 0.0)
        a_off = lax.dot_general(q_off[:, sl], k_off[:, sl], NT_DIMS, preferred_element_type=F32)
        attn.append(cat([a_dg[:sp], jnp.concatenate([a_off, a_dg[sp:, sp:]], axis=1)]).astype(BF16))
    between(2)
    outs, new_states = [], []
    for h in heads:
        sl, st = lanes[h], states[h]
        o_h = (jnp.dot(attn[h], vx[:, sl], preferred_element_type=F32)
               + lax.dot_general(q_st[:, sl], st.astype(BF16), NT_DIMS, preferred_element_type=F32))
        new_states.append(st * dec[:, sl] + lax.dot_general(vx[:, sl], k_st[:, sl], TN_DIMS,
                                                            preferred_element_type=F32))
        ms = jnp.mean(o_h * o_h, axis=-1, keepdims=True)
        outs.append(o_h * lax.rsqrt(ms + EPS))
    return jnp.concatenate(outs, axis=1), new_states


def _hgrn_layer_kernel(h_ref, w_ref, gates_ref, kv_ref, *rest, layer, sub, n_mix_steps,
                       side_columns):
    n_side = len(side_columns)
    side_src, (bm_ref, bc_ref), rest = rest[:n_side], rest[n_side:n_side + 2], rest[n_side + 2:]
    side_dst, (st_ref,) = rest[:n_side], rest[n_side:]
    s = pl.program_id(1)
    j = pl.program_id(2)
    tm = h_ref.shape[0]
    n_sub = tm // sub
    W = bm_ref.shape[1]
    heads = W // HG_HEAD_DIM

    def project(r, w):
        return jnp.dot(h_ref[r * sub:(r + 1) * sub, :], w, preferred_element_type=F32)

    @pl.when((j == 0) & (s == 0))
    def _():
        st_ref[...] = jnp.zeros_like(st_ref)

    @pl.when(j < n_mix_steps)
    def _():
        _side_cast(side_src, side_dst, side_columns)
        sm = jax.nn.softmax(gates_ref[:DEPTH, :], axis=0)
        lb = jnp.sum(sm[:layer + 1], axis=0, keepdims=True) - sm[0:1]
        norm_gain = gates_ref[DEPTH:, :]
        r_i = lax.broadcasted_iota(jnp.int32, (sub, sub), 0)
        c_i = lax.broadcasted_iota(jnp.int32, (sub, sub), 1)
        causal = (r_i // HG_SPAN == c_i // HG_SPAN) & (c_i <= r_i)
        states = [st_ref[j * heads + h] for h in range(heads)]
        n_part = 4
        part = lambda k: w_ref[:, k * W:(k + 1) * W]
        cur = [project(0, part(k)) for k in range(n_part)]
        for r in range(n_sub):
            nxt = [None] * n_part

            def between(k, r=r, nxt=nxt):
                if r + 1 < n_sub:
                    nxt[k] = project(r + 1, part(k))

            qx, fx, ix, gx = cur
            o, states = _hgrn_rows(qx, fx, ix.astype(BF16), lb, states, causal, between)
            between(3)
            bm_ref[r * sub:(r + 1) * sub, :] = ((o * norm_gain) * _silu(gx)).astype(BF16)
            cur = nxt
        for h in range(heads):
            st_ref[j * heads + h] = states[h]

    @pl.when(j >= n_mix_steps)
    def _():
        _side_cast(side_src, side_dst, side_columns)
        _attention_step(project, w_ref, kv_ref, bc_ref, n_sub, sub)


def _hgrn_weight_columns():
    W = HG_HEADS_PER_STEP * HG_HEAD_DIM
    wc = CA_HEADS_PER_STEP * CA_HEAD_DIM
    gate0 = 3 * E_MIX + E_CA
    return _step_major_columns((0, E_MIX, 2 * E_MIX, gate0), W, E_MIX // W,
                               (3 * E_MIX, gate0 + E_MIX), wc, E_CA // wc)


def _hgrn_layer(h, w_in, gates, kv, side, *, kv_col0, layer, batch, seq, n_mem, tm, sub):
    M, D = h.shape
    W = HG_HEADS_PER_STEP * HG_HEAD_DIM
    wc = CA_HEADS_PER_STEP * CA_HEAD_DIM
    tn = 4 * W
    assert seq % tm == 0 and tm % sub == 0 and sub == 2 * HG_SPAN
    assert tn == 2 * wc and kv_col0 % (2 * wc) == 0
    n_mix, n_ca = E_MIX // W, E_CA // wc
    spt = seq // tm
    mix_blk = lambda j: jnp.minimum(j, n_mix - 1)
    ca_blk = lambda j: jnp.clip(j - n_mix, 0, n_ca - 1)
    blocks = [((tm, D), BF16), ((D, tn), BF16), ((DEPTH + 1, W), F32), ((n_mem, 2 * wc), BF16),
              ((tm, W), BF16), ((tm, wc), BF16)]
    scratch = [((E_MIX // HG_HEAD_DIM, HG_HEAD_DIM, HG_HEAD_DIM), F32)]
    n_steps = n_mix + n_ca
    side_in, side_out, side_shapes, side_blk, side_columns = _side_cast_plan(
        side, lambda b, s, j: (b * spt + s) * n_steps + j, batch * spt * n_steps)
    return pl.pallas_call(
        functools.partial(_hgrn_layer_kernel, layer=layer, sub=sub, n_mix_steps=n_mix,
                          side_columns=side_columns),
        grid=(batch, spt, n_steps),
        in_specs=[pl.BlockSpec((tm, D), lambda b, s, j: (b * spt + s, 0)),
                  pl.BlockSpec((D, tn), lambda b, s, j: (0, j)),
                  pl.BlockSpec((DEPTH + 1, W), lambda b, s, j: (0, mix_blk(j))),
                  pl.BlockSpec((n_mem, 2 * wc),
                               lambda b, s, j: (b, kv_col0 // (2 * wc) + ca_blk(j)))] + side_in,
        out_specs=[pl.BlockSpec((tm, W), lambda b, s, j: (b * spt + s, mix_blk(j))),
                   pl.BlockSpec((tm, wc), lambda b, s, j: (b * spt + s, ca_blk(j)))] + side_out,
        out_shape=[jax.ShapeDtypeStruct((M, E_MIX), BF16),
                   jax.ShapeDtypeStruct((M, E_CA), BF16)] + side_shapes,
        scratch_shapes=[pltpu.VMEM(s_, d_) for s_, d_ in scratch],
        compiler_params=_compiler_params(("arbitrary", "arbitrary", "arbitrary"),
                                         blocks + side_blk, scratch,
                                         temps=32 * _nbytes((sub, W), F32)),
        name="hgrn_layer",
    )(h, w_in, gates, kv, *[a for a, _, _, _ in side])


def _out_proj_kernel(x_ref, bm_ref, bc_ref, wm_ref, wc_ref, g_ref, *out_refs, last):
    acc = jnp.dot(bm_ref[...], wm_ref[...], preferred_element_type=F32)
    acc = acc + jnp.dot(bc_ref[...], wc_ref[...], preferred_element_type=F32)
    y = x_ref[...] + acc
    normed = _rms_rows(y, g_ref[...])
    if last:
        out_refs[0][...] = normed
    else:
        out_refs[0][...] = y
        out_refs[1][...] = normed.astype(BF16)


def _out_proj(x, bm, bc, w_out, g, *, tm, last):
    M, D = x.shape
    assert M % tm == 0
    resident = pl.Buffered(1)
    row_blk = lambda w: pl.BlockSpec((tm, w), lambda i: (i, 0))
    out_specs, out_shape = [row_blk(D)], [jax.ShapeDtypeStruct((M, D), F32)]
    blocks = [((tm, D), F32), ((tm, E_MIX), BF16), ((tm, E_CA), BF16), ((1, D), F32), ((tm, D), F32)]
    if not last:
        out_specs.append(row_blk(D))
        out_shape.append(jax.ShapeDtypeStruct((M, D), BF16))
        blocks.append(((tm, D), BF16))
    single = [((E_MIX, D), BF16), ((E_CA, D), BF16)]
    return pl.pallas_call(
        functools.partial(_out_proj_kernel, last=last),
        grid=(M // tm,),
        in_specs=[row_blk(D), row_blk(E_MIX), row_blk(E_CA),
                  pl.BlockSpec((E_MIX, D), lambda i: (0, 0), pipeline_mode=resident),
                  pl.BlockSpec((E_CA, D), lambda i: (E_MIX // E_CA, 0), pipeline_mode=resident),
                  pl.BlockSpec((1, D), lambda i: (0, 0))],
        out_specs=out_specs,
        out_shape=out_shape,
        compiler_params=_compiler_params(("parallel",), blocks, single,
                                         temps=2 * _nbytes((tm, D), F32)),
        name="out_proj_last" if last else "out_proj",
    )(x, bm, bc, w_out, w_out, g.reshape(1, D))


def kernel(x, mem, norm_g, mem_norm_g, w_kv, w_out, pool_w_in, pool_w_grp, pool_scale,
           hgrn_w_in, hgrn_lb, hgrn_norm_g, final_g):
    batch, seq, d_model = x.shape
    n_mem = mem.shape[1]
    x2 = x.reshape(batch * seq, d_model)
    mem2 = mem.reshape(batch * n_mem, d_model)
    assert DEPTH == 2 and N_MIXERS == 2
    kv, w_pool = _kv_proj(
        mem2, mem_norm_g, w_kv, [(pool_w_in, 0, POOL_W_CAST_ROWS, _pool_weight_columns())],
        tm=KV_PROJ_ROWS, tn=CA_HEADS_PER_STEP * CA_HEAD_DIM)
    common = dict(batch=batch, seq=seq, n_mem=n_mem, sub=SUB_ROWS)
    bm, bc, w_hg, w_o0 = _pool_layer(
        x2, norm_g[0], w_pool, pool_w_grp[0], pool_scale[0], kv,
        [(hgrn_w_in, 0, HGRN_W_CAST_ROWS, _hgrn_weight_columns()),
         (w_out, 0, W_OUT_CAST_ROWS, None)],
        kv_col0=0, tm=POOL_LAYER_ROWS, **common)
    x2, h = _out_proj(x2, bm, bc, w_o0, norm_g[1], tm=OUT_PROJ_ROWS, last=False)
    gates = jnp.concatenate([hgrn_lb, hgrn_norm_g[:1]], axis=0)
    bm, bc, w_o1 = _hgrn_layer(h, w_hg, gates, kv, [(w_out, 1, W_OUT_CAST_ROWS, None)],
                               kv_col0=2 * E_CA, layer=1, tm=HGRN_LAYER_ROWS, **common)
    y, = _out_proj(x2, bm, bc, w_o1, final_g, tm=OUT_PROJ_ROWS, last=True)
    return y.reshape(batch, seq, d_model)
```
